```python
import jax, jax.numpy as jnp
from jax import lax
import numpy as np

D_MODEL = 1024
BATCH = 4
SEQ = 8192
DEPTH = 2

N_HEADS = 8
QK_NOPE_DIM = 64
QK_ROPE_DIM = 32
V_HEAD_DIM = 64
Q_LORA_RANK = 384
KV_LORA_RANK = 256
ROPE_BASE = 10000.0
Q_BLOCK = 128
QK_DIM = QK_NOPE_DIM + QK_ROPE_DIM
ATTN_WIDTH = N_HEADS * V_HEAD_DIM
CONV_CHANNELS = 512
CONV_KERNEL = 31
N_GROUPS = 4
EXPERTS_PER_GROUP = 8
TOP_K_IN_GROUP = 2
D_EXPERT = 128
EPS = 1e-6
SPLIT_POINTS = tuple(int(v) for v in np.cumsum([Q_LORA_RANK, KV_LORA_RANK, QK_ROPE_DIM, 2 * CONV_CHANNELS]))
IN_COLS = Q_LORA_RANK + KV_LORA_RANK + QK_ROPE_DIM + 2 * CONV_CHANNELS + 2 * D_MODEL

kernel_name = "hybrid_mla_conformer_hmoe_adaln"


def rmsnorm(x, g):
    xf = x.astype(jnp.float32)
    y = xf * lax.rsqrt(jnp.mean(xf * xf, axis=-1, keepdims=True) + EPS)
    return (y * g.astype(jnp.float32)).astype(x.dtype)


def layernorm(x, g, b):
    xf = x.astype(jnp.float32)
    mu = jnp.mean(xf, axis=-1, keepdims=True)
    var = jnp.mean(jnp.square(xf - mu), axis=-1, keepdims=True)
    y = (xf - mu) * lax.rsqrt(var + EPS)
    return (y * g.astype(jnp.float32) + b.astype(jnp.float32)).astype(x.dtype)


def rope_tables(positions):
    inv_freq = ROPE_BASE ** (-jnp.arange(0, QK_ROPE_DIM, 2, dtype=jnp.float32) / QK_ROPE_DIM)
    ang = positions.astype(jnp.float32)[..., None] * inv_freq
    return jnp.cos(ang), jnp.sin(ang)


def apply_rope(x, cos, sin):
    xf = x.astype(jnp.float32)
    x1, x2 = jnp.split(xf, 2, axis=-1)
    return jnp.concatenate([x1 * cos - x2 * sin, x2 * cos + x1 * sin], axis=-1).astype(x.dtype)


def causal_block_attention(q, k, v, positions):
    B, H, S, _ = q.shape
    nb = S // Q_BLOCK
    scale = QK_DIM ** -0.5
    qb = q.reshape(B, H, nb, Q_BLOCK, QK_DIM).transpose(2, 0, 1, 3, 4)
    pb = positions.reshape(B, nb, Q_BLOCK).transpose(1, 0, 2)
    neg = jnp.finfo(jnp.float32).min

    def one_block(args):
        q_blk, p_blk = args
        s = jnp.einsum('bhqd,bhkd->bhqk', q_blk, k, preferred_element_type=jnp.float32) * scale
        mask = positions[:, None, None, :] <= p_blk[:, None, :, None]
        p = jax.nn.softmax(jnp.where(mask, s, neg), axis=-1)
        return jnp.einsum('bhqk,bhkd->bhqd', p.astype(v.dtype), v)

    out = lax.map(one_block, (qb, pb))
    return out.transpose(1, 0, 3, 2, 4).reshape(B, S, H * V_HEAD_DIM)


def causal_depthwise_conv(u, w, b):
    y = lax.conv_general_dilated(
        u, w[:, None, :].astype(u.dtype), window_strides=(1,),
        padding=[(CONV_KERNEL - 1, 0)], dimension_numbers=('NWC', 'WIO', 'NWC'),
        feature_group_count=u.shape[-1])
    return y + b


def hierarchical_moe(h, rg_w, rg_b, re_w, re_b, e_gate, e_up, e_down):
    B, S, D = h.shape
    t = h.reshape(-1, D)
    g_prob = jax.nn.softmax((t @ rg_w + rg_b).astype(jnp.float32), axis=-1)
    g_val, g_idx = lax.top_k(g_prob, 1)
    e_logits = (t @ re_w + re_b).astype(jnp.float32).reshape(-1, N_GROUPS, EXPERTS_PER_GROUP)
    e_sel = jnp.take_along_axis(e_logits, g_idx[:, :, None], axis=1)[:, 0]
    e_val, e_idx = lax.top_k(jax.nn.softmax(e_sel, axis=-1), TOP_K_IN_GROUP)
    e_val = e_val / jnp.sum(e_val, axis=-1, keepdims=True)
    w_group = jnp.sum(jax.nn.one_hot(e_idx, EXPERTS_PER_GROUP, dtype=jnp.float32) * e_val[..., None], axis=1)
    combine = (jax.nn.one_hot(g_idx[:, 0], N_GROUPS, dtype=jnp.float32)[:, :, None]
               * (g_val * w_group)[:, None, :]).astype(h.dtype)
    out = jnp.zeros_like(t)
    for gi in range(N_GROUPS):
        a = jnp.einsum('td,edf->tef', t, e_gate[gi])
        u = jnp.einsum('td,edf->tef', t, e_up[gi])
        hid = jax.nn.silu(a) * u * combine[:, gi, :, None]
        out = out + jnp.einsum('tef,efd->td', hid, e_down[gi])
    return out.reshape(B, S, D)


def setup_inputs(seed: int = 0) -> dict:
    key = jax.random.key(seed)
    ks = iter(jax.random.split(key, 32))
    L, D = DEPTH, D_MODEL

    def nrm(shape, scale):
        return jax.random.normal(next(ks), shape, jnp.float32) * scale

    x = nrm((BATCH, SEQ, D), 1.0)
    c = nrm((BATCH, D), 1.0)
    offsets = jax.random.randint(next(ks), (BATCH, 1), 0, 1024, dtype=jnp.int32)
    positions = offsets + jnp.arange(SEQ, dtype=jnp.int32)[None, :]
    G, E, F = N_GROUPS, EXPERTS_PER_GROUP, D_EXPERT
    return {
        "x": x, "c": c, "positions": positions,
        "ada_w": nrm((L, D, 6 * D), 0.5 * D ** -0.5),
        "ada_b": nrm((L, 6 * D), 0.02),
        "norm1_g": 1.0 + nrm((L, D), 0.02),
        "norm2_g": 1.0 + nrm((L, D), 0.02),
        "w_in": nrm((L, D, IN_COLS), D ** -0.5),
        "q_norm_g": 1.0 + nrm((L, Q_LORA_RANK), 0.02),
        "w_uq": nrm((L, Q_LORA_RANK, N_HEADS * QK_DIM), Q_LORA_RANK ** -0.5),
        "kv_norm_g": 1.0 + nrm((L, KV_LORA_RANK), 0.02),
        "w_ukv": nrm((L, KV_LORA_RANK, N_HEADS * (QK_NOPE_DIM + V_HEAD_DIM)), KV_LORA_RANK ** -0.5),
        "w_o_attn": nrm((L, ATTN_WIDTH, D), ATTN_WIDTH ** -0.5),
        "conv_w": nrm((L, CONV_KERNEL, CONV_CHANNELS), CONV_KERNEL ** -0.5),
        "conv_b": nrm((L, CONV_CHANNELS), 0.02),
        "conv_ln_g": 1.0 + nrm((L, CONV_CHANNELS), 0.02),
        "conv_ln_b": nrm((L, CONV_CHANNELS), 0.02),
        "w_conv_out": nrm((L, CONV_CHANNELS, D), CONV_CHANNELS ** -0.5),
        "w_out": nrm((L, D, D), D ** -0.5),
        "router_group_w": nrm((L, D, G), D ** -0.5),
        "router_group_b": nrm((L, G), 0.01),
        "router_expert_w": nrm((L, D, G * E), D ** -0.5),
        "router_expert_b": nrm((L, G * E), 0.01),
        "expert_w_gate": nrm((L, G, E, D, F), D ** -0.5),
        "expert_w_up": nrm((L, G, E, D, F), D ** -0.5),
        "expert_w_down": nrm((L, G, E, F, D), F ** -0.5),
        "final_norm_g": 1.0 + nrm((D,), 0.02),
    }


def reference(x, c, positions, ada_w, ada_b, norm1_g, norm2_g, w_in, q_norm_g, w_uq,
              kv_norm_g, w_ukv, w_o_attn, conv_w, conv_b, conv_ln_g, conv_ln_b,
              w_conv_out, w_out, router_group_w, router_group_b, router_expert_w,
              router_expert_b, expert_w_gate, expert_w_up, expert_w_down, final_norm_g):
    B, S, D = x.shape
    cos, sin = rope_tables(positions)
    c_act = jax.nn.silu(c)
    for l in range(DEPTH):
        ada = c_act @ ada_w[l] + ada_b[l]
        sh1, sc1, gt1, sh2, sc2, gt2 = [a[:, None, :] for a in jnp.split(ada, 6, axis=-1)]

        h = rmsnorm(x, norm1_g[l]) * (1.0 + sc1) + sh1
        z = h @ w_in[l]
        q_lat, kv_lat, k_rope, conv_in, gate_logits = jnp.split(z, SPLIT_POINTS, axis=-1)

        q = (rmsnorm(q_lat, q_norm_g[l]) @ w_uq[l]).reshape(B, S, N_HEADS, QK_DIM)
        q_nope, q_pe = jnp.split(q, [QK_NOPE_DIM], axis=-1)
        q_pe = apply_rope(q_pe, cos[:, :, None, :], sin[:, :, None, :])
        kv = (rmsnorm(kv_lat, kv_norm_g[l]) @ w_ukv[l]).reshape(B, S, N_HEADS, QK_NOPE_DIM + V_HEAD_DIM)
        k_nope, v = jnp.split(kv, [QK_NOPE_DIM], axis=-1)
        k_pe = jnp.broadcast_to(apply_rope(k_rope, cos, sin)[:, :, None, :], (B, S, N_HEADS, QK_ROPE_DIM))
        qh = jnp.concatenate([q_nope, q_pe], axis=-1).transpose(0, 2, 1, 3)
        kh = jnp.concatenate([k_nope, k_pe], axis=-1).transpose(0, 2, 1, 3)
        attn = causal_block_attention(qh, kh, v.transpose(0, 2, 1, 3), positions)
        y_attn = attn @ w_o_attn[l]

        glu_a, glu_b = jnp.split(conv_in, 2, axis=-1)
        u = glu_a * jax.nn.sigmoid(glu_b)
        u = causal_depthwise_conv(u, conv_w[l], conv_b[l])
        u = jax.nn.silu(layernorm(u, conv_ln_g[l], conv_ln_b[l]))
        y_conv = u @ w_conv_out[l]

        g_attn, g_conv = jnp.split(jax.nn.sigmoid(gate_logits), 2, axis=-1)
        y = (g_attn * y_attn + g_conv * y_conv) @ w_out[l]
        x = x + gt1 * y

        h = rmsnorm(x, norm2_g[l]) * (1.0 + sc2) + sh2
        x = x + gt2 * hierarchical_moe(h, router_group_w[l], router_group_b[l], router_expert_w[l],
                                       router_expert_b[l], expert_w_gate[l], expert_w_up[l],
                                       expert_w_down[l])
    return rmsnorm(x, final_norm_g)
```

```python
import functools
import math

import jax
import jax.numpy as jnp
from jax import lax
from jax.experimental import pallas as pl
from jax.experimental.pallas import tpu as pltpu

D_MODEL = 1024
N_HEADS = 8
QK_NOPE = 64
QK_ROPE = 32
V_DIM = 64
QK_DIM = QK_NOPE + QK_ROPE
Q_LORA = 384
KV_LORA = 256
ROPE_BASE = 10000.0
CONV_C = 512
CONV_K = 31
N_GROUPS = 4
E_PER_G = 8
D_EXPERT = 128
EPS = 1e-6

LANE = 128
HEAD_PAD = LANE
KPE_OFF = QK_NOPE
HALO = 32
VMEM_LIMIT = 56 * 1024 * 1024

C_QLAT = 0
C_KVLAT = Q_LORA
C_KPE = Q_LORA + KV_LORA
C_GLU_A = C_KPE + LANE
C_GLU_B = C_GLU_A + CONV_C
C_GATE = C_GLU_B + CONV_C
IN_COLS_PAD = C_GATE + 2 * D_MODEL

TM_IN = 512
TM_MIX = 512
TM_MOE = 512
TQ = 512
TK = 512

BF16 = jnp.bfloat16
F32 = jnp.float32


def _cparams(sem):
    return pltpu.CompilerParams(dimension_semantics=sem, vmem_limit_bytes=VMEM_LIMIT)


def _ada_kernel(c_ref, w_ref, b_ref, o_ref):
    c = c_ref[...]
    c_act = c * jax.nn.sigmoid(c)
    o_ref[0] = jnp.dot(c_act, w_ref[0], preferred_element_type=F32,
                       precision=lax.Precision.HIGHEST) + b_ref[0]


def _ada_call(c_pad, ada_w, ada_b):
    L, D, N = ada_w.shape
    tn = 1536
    return pl.pallas_call(
        _ada_kernel,
        grid=(L, N // tn),
        in_specs=[
            pl.BlockSpec((8, D), lambda l, j: (0, 0)),
            pl.BlockSpec((1, D, tn), lambda l, j: (l, 0, j)),
            pl.BlockSpec((1, 1, tn), lambda l, j: (l, 0, j)),
        ],
        out_specs=pl.BlockSpec((1, 8, tn), lambda l, j: (l, 0, j)),
        out_shape=jax.ShapeDtypeStruct((L, 8, N), F32),
        compiler_params=_cparams(("arbitrary", "arbitrary")),
        name="ada_ln",
    )(c_pad, ada_w, ada_b.reshape(L, 1, N))


def _rope_kernel(pos_ref, freq_ref, c_ref, s1_ref, s2_ref):
    pos = pos_ref[...].astype(F32)
    ang = pos * freq_ref[...]
    lane = lax.broadcasted_iota(jnp.int32, ang.shape, 1)
    cosv = jnp.cos(ang)
    sinv = jnp.sin(ang)
    is_x1 = (lane >= KPE_OFF) & (lane < KPE_OFF + QK_ROPE // 2)
    is_x2 = (lane >= KPE_OFF + QK_ROPE // 2) & (lane < KPE_OFF + QK_ROPE)
    c_ref[...] = jnp.where(is_x1 | is_x2, cosv, jnp.where(lane < KPE_OFF, 1.0, 0.0))
    s1_ref[...] = jnp.where(is_x2, sinv, 0.0)
    s2_ref[...] = jnp.where(is_x1, -sinv, 0.0)


def _rope_call(pos_col, freq_row):
    T = pos_col.shape[0]
    tm = 2048
    spec = pl.BlockSpec((tm, LANE), lambda i: (i, 0))
    return pl.pallas_call(
        _rope_kernel,
        grid=(T // tm,),
        in_specs=[pl.BlockSpec((tm, 1), lambda i: (i, 0)),
                  pl.BlockSpec((1, LANE), lambda i: (0, 0))],
        out_specs=[spec, spec, spec],
        out_shape=[jax.ShapeDtypeStruct((T, LANE), F32)] * 3,
        compiler_params=_cparams(("arbitrary",)),
        name="rope_tables",
    )(pos_col, freq_row)


def _rms(x, g):
    ms = jnp.mean(x * x, axis=-1, keepdims=True)
    return x * lax.rsqrt(ms + EPS) * g


def _rope_apply(x, c, s1, s2):
    half = QK_ROPE // 2
    return x * c + pltpu.roll(x, half, axis=1) * s1 + pltpu.roll(x, LANE - half, axis=1) * s2


def _inproj_kernel(x_ref, ada_ref, g1_ref, win_ref, qg_ref, wuq_ref, kvg_ref, wuk_ref, wuv_ref,
                   c_ref, s1_ref, s2_ref, q_ref, k_ref, v_ref, u_ref, gate_ref):
    x = x_ref[...]
    sh1 = ada_ref[0, 0:1, :]
    sc1 = ada_ref[0, 1:2, :]
    h = _rms(x, g1_ref[...]) * (1.0 + sc1) + sh1
    z = jnp.dot(h.astype(BF16), win_ref[...], preferred_element_type=F32)

    qn = _rms(z[:, C_QLAT:C_QLAT + Q_LORA], qg_ref[...]).astype(BF16)
    kvn = _rms(z[:, C_KVLAT:C_KVLAT + KV_LORA], kvg_ref[...]).astype(BF16)
    q = jnp.dot(qn, wuq_ref[...], preferred_element_type=F32)
    kk = jnp.dot(kvn, wuk_ref[...], preferred_element_type=F32)
    vv = jnp.dot(kvn, wuv_ref[...], preferred_element_type=F32)

    c = c_ref[...]
    s1 = s1_ref[...]
    s2 = s2_ref[...]
    qs = (QK_DIM ** -0.5) * math.log2(math.e)
    cq, s1q, s2q = c * qs, s1 * qs, s2 * qs
    kpe = _rope_apply(z[:, C_KPE:C_KPE + LANE], c, s1, s2)
    lane = lax.broadcasted_iota(jnp.int32, (1, LANE), 1)
    ones_col = jnp.where(lane == V_DIM, 1.0, 0.0)
    for hh in range(N_HEADS):
        sl = slice(hh * HEAD_PAD, (hh + 1) * HEAD_PAD)
        q_ref[0, hh] = _rope_apply(q[:, sl], cq, s1q, s2q).astype(BF16)
        k_ref[0, hh] = (kk[:, sl] + kpe).astype(BF16)
        v_ref[0, hh] = (vv[:, sl] + ones_col).astype(BF16)

    u_ref[...] = (z[:, C_GLU_A:C_GLU_A + CONV_C]
                  * jax.nn.sigmoid(z[:, C_GLU_B:C_GLU_B + CONV_C])).astype(BF16)
    gate_ref[...] = jax.nn.sigmoid(z[:, C_GATE:C_GATE + 2 * D_MODEL]).astype(BF16)


def _inproj_call(x2d, ada_l, g1, w_in_p, qg, w_uq_p, kvg, w_uk_p, w_uv_p, tc, ts1, ts2, B, S):
    T, D = x2d.shape
    tm = TM_IN
    tpb = S // tm
    const = lambda shape: pl.BlockSpec(shape, lambda i: (0,) * len(shape))
    head_spec = pl.BlockSpec((1, N_HEADS, tm, HEAD_PAD), lambda i: (i // tpb, 0, i % tpb, 0))
    head_shape = jax.ShapeDtypeStruct((B, N_HEADS, S, HEAD_PAD), BF16)
    tab = pl.BlockSpec((tm, LANE), lambda i: (i, 0))
    return pl.pallas_call(
        _inproj_kernel,
        grid=(T // tm,),
        in_specs=[
            pl.BlockSpec((tm, D), lambda i: (i, 0)),
            pl.BlockSpec((1, 6, D), lambda i: (i // tpb, 0, 0)),
            const((1, D)),
            const((D, IN_COLS_PAD)),
            const((1, Q_LORA)),
            const((Q_LORA, N_HEADS * HEAD_PAD)),
            const((1, KV_LORA)),
            const((KV_LORA, N_HEADS * HEAD_PAD)),
            const((KV_LORA, N_HEADS * HEAD_PAD)),
            tab, tab, tab,
        ],
        out_specs=[head_spec, head_spec, head_spec,
                   pl.BlockSpec((tm, CONV_C), lambda i: (i, 0)),
                   pl.BlockSpec((tm, 2 * D_MODEL), lambda i: (i, 0))],
        out_shape=[head_shape, head_shape, head_shape,
                   jax.ShapeDtypeStruct((T, CONV_C), BF16),
                   jax.ShapeDtypeStruct((T, 2 * D_MODEL), BF16)],
        compiler_params=_cparams(("arbitrary",)),
        name="in_proj",
    )(x2d, ada_l, g1, w_in_p, qg, w_uq_p, kvg, w_uk_p, w_uv_p, tc, ts1, ts2)


HEADS_PER_STEP = 2


def _attn_kernel(q_ref, k_ref, v_ref, o_ref):
    qi = pl.program_id(2)
    outs = []
    for hh in range(HEADS_PER_STEP):
        q = q_ref[0, hh]

        def scores(kb):
            k = k_ref[0, hh, pl.ds(kb * TK, TK), :]
            return lax.dot_general(q, k, (((1,), (1,)), ((), ())), preferred_element_type=F32)

        def update(s, kb, m, acc):
            v = v_ref[0, hh, pl.ds(kb * TK, TK), :]
            m_new = jnp.maximum(m, jnp.max(s, axis=-1, keepdims=True))
            p = jnp.exp2(s - m_new)
            alpha = jnp.exp2(m - m_new)
            acc = acc * alpha + jnp.dot(p.astype(BF16), v, preferred_element_type=F32)
            return m_new, acc

        def body(kb, carry):
            m, acc = carry
            return update(scores(kb), kb, m, acc)

        m0 = jnp.full((TQ, 1), -jnp.inf, F32)
        acc0 = jnp.zeros((TQ, HEAD_PAD), F32)
        m, acc = lax.fori_loop(0, qi, body, (m0, acc0))
        s = scores(qi)
        row = lax.broadcasted_iota(jnp.int32, (TQ, TK), 0)
        col = lax.broadcasted_iota(jnp.int32, (TQ, TK), 1)
        s = jnp.where(col <= row, s, -jnp.inf)
        m, acc = update(s, qi, m, acc)
        outs.append(acc[:, :V_DIM] / acc[:, V_DIM:V_DIM + 1])
    o_ref[0] = jnp.concatenate(outs, axis=-1).astype(BF16)


def _attn_call(q, k, v):
    B, H, S, _ = q.shape
    nq = S // TQ
    hs = HEADS_PER_STEP
    return pl.pallas_call(
        _attn_kernel,
        grid=(B, H // hs, nq),
        in_specs=[
            pl.BlockSpec((1, hs, TQ, HEAD_PAD), lambda b, h, i: (b, h, i, 0)),
            pl.BlockSpec((1, hs, S, HEAD_PAD), lambda b, h, i: (b, h, 0, 0)),
            pl.BlockSpec((1, hs, S, HEAD_PAD), lambda b, h, i: (b, h, 0, 0)),
        ],
        out_specs=pl.BlockSpec((1, TQ, hs * V_DIM), lambda b, h, i: (b, i, h)),
        out_shape=jax.ShapeDtypeStruct((B, S, H * V_DIM), BF16),
        compiler_params=_cparams(("arbitrary", "arbitrary", "arbitrary")),
        name="mla_attention",
    )(q, k, v)


def _mix_kernel(attn_ref, u_ref, halo_ref, gate_ref, x_ref, ada_ref, woa_ref, cw_ref, cb_ref,
                lng_ref, lnb_ref, wco_ref, wout_ref, g2_ref, wr_ref, br_ref,
                x1_ref, h2_ref, comb_ref, buf_ref, *, tiles_per_batch):
    i = pl.program_id(0)
    tm = u_ref.shape[0]
    first = (i % tiles_per_batch) == 0

    buf_ref[0:HALO, :] = jnp.where(first, 0.0, halo_ref[...].astype(F32))
    buf_ref[HALO:HALO + tm, :] = u_ref[...].astype(F32)
    acc = jnp.zeros((tm, CONV_C), F32) + cb_ref[...]
    base = HALO - (CONV_K - 1)
    for j in range(CONV_K):
        acc = acc + buf_ref[pl.ds(base + j, tm), :] * cw_ref[j:j + 1, :]
    mu = jnp.mean(acc, axis=-1, keepdims=True)
    xc = acc - mu
    var = jnp.mean(xc * xc, axis=-1, keepdims=True)
    ln = xc * lax.rsqrt(var + EPS) * lng_ref[...] + lnb_ref[...]
    uc = ln * jax.nn.sigmoid(ln)
    y_conv = jnp.dot(uc.astype(BF16), wco_ref[...], preferred_element_type=F32)
    y_attn = jnp.dot(attn_ref[...], woa_ref[...], preferred_element_type=F32)

    g_attn = gate_ref[:, 0:D_MODEL].astype(F32)
    g_conv = gate_ref[:, D_MODEL:2 * D_MODEL].astype(F32)
    merged = g_attn * y_attn + g_conv * y_conv
    y = jnp.dot(merged.astype(BF16), wout_ref[...], preferred_element_type=F32)
    gt1 = ada_ref[0, 2:3, :]
    x1 = x_ref[...] + gt1 * y
    x1_ref[...] = x1

    sh2 = ada_ref[0, 3:4, :]
    sc2 = ada_ref[0, 4:5, :]
    h2 = (_rms(x1, g2_ref[...]) * (1.0 + sc2) + sh2).astype(BF16)
    h2_ref[...] = h2

    logits = jnp.dot(h2, wr_ref[...], preferred_element_type=F32) + br_ref[...]
    lane = lax.broadcasted_iota(jnp.int32, logits.shape, 1)
    big = jnp.int32(1 << 20)
    neg = -jnp.inf
    lg = jnp.where(lane < N_GROUPS, logits, neg)
    gmax = jnp.max(lg, axis=-1, keepdims=True)
    g_val = 1.0 / jnp.sum(jnp.exp(lg - gmax), axis=-1, keepdims=True)
    g_idx = jnp.min(jnp.where(lg == gmax, lane, big), axis=-1, keepdims=True)
    e_lo = N_GROUPS + g_idx * E_PER_G
    emask = (lane >= e_lo) & (lane < e_lo + E_PER_G)
    le = jnp.where(emask, logits, neg)
    m1 = jnp.max(le, axis=-1, keepdims=True)
    i1 = jnp.min(jnp.where(le == m1, lane, big), axis=-1, keepdims=True)
    le2 = jnp.where(lane == i1, neg, le)
    m2 = jnp.max(le2, axis=-1, keepdims=True)
    i2 = jnp.min(jnp.where(le2 == m2, lane, big), axis=-1, keepdims=True)
    z_e = jnp.sum(jnp.exp(le - m1), axis=-1, keepdims=True)
    p1 = 1.0 / z_e
    p2 = jnp.exp(m2 - m1) / z_e
    w1 = g_val * (p1 / (p1 + p2))
    w2 = g_val * (p2 / (p1 + p2))
    comb = jnp.where(lane == i1, w1, 0.0) + jnp.where(lane == i2, w2, 0.0)
    for g in range(N_GROUPS):
        shifted = pltpu.roll(comb, LANE - (N_GROUPS + g * E_PER_G), axis=1)
        comb_ref[:, g * LANE:(g + 1) * LANE] = jnp.where(lane < E_PER_G, shifted, 0.0)


def _mix_call(attn2d, u2d, gates, x2d, ada_l, woa, cw, cb, lng, lnb, wco, wout, g2, wr, br, S):
    T, D = x2d.shape
    tm = TM_MIX
    tpb = S // tm
    hb = tm // HALO
    const = lambda shape: pl.BlockSpec(shape, lambda i: (0,) * len(shape))
    row = lambda w: pl.BlockSpec((tm, w), lambda i: (i, 0))
    return pl.pallas_call(
        functools.partial(_mix_kernel, tiles_per_batch=tpb),
        grid=(T // tm,),
        in_specs=[
            row(N_HEADS * V_DIM),
            row(CONV_C),
            pl.BlockSpec((HALO, CONV_C), lambda i: (jnp.maximum(i * hb - 1, 0), 0)),
            row(2 * D),
            row(D),
            pl.BlockSpec((1, 6, D), lambda i: (i // tpb, 0, 0)),
            const((N_HEADS * V_DIM, D)),
            const((HALO, CONV_C)),
            const((1, CONV_C)),
            const((1, CONV_C)),
            const((1, CONV_C)),
            const((CONV_C, D)),
            const((D, D)),
            const((1, D)),
            const((D, LANE)),
            const((1, LANE)),
        ],
        out_specs=[row(D), row(D), row(N_GROUPS * LANE)],
        out_shape=[jax.ShapeDtypeStruct((T, D), F32),
                   jax.ShapeDtypeStruct((T, D), BF16),
                   jax.ShapeDtypeStruct((T, N_GROUPS * LANE), F32)],
        scratch_shapes=[pltpu.VMEM((HALO + tm, CONV_C), F32)],
        compiler_params=_cparams(("arbitrary",)),
        name="mixer_merge",
    )(attn2d, u2d, u2d, gates, x2d, ada_l, woa, cw, cb, lng, lnb, wco, wout, g2, wr, br)


def _moe_kernel(h_ref, comb_ref, x1_ref, ada_ref, wg_ref, wu_ref, wd_ref, gf_ref, o_ref, acc_ref,
                *, final_norm):
    g = pl.program_id(1)

    @pl.when(g == 0)
    def _():
        acc_ref[...] = jnp.zeros_like(acc_ref)

    h = h_ref[...]
    a = jnp.dot(h, wg_ref[0], preferred_element_type=F32)
    u = jnp.dot(h, wu_ref[0], preferred_element_type=F32)
    cw = comb_ref[...]
    tm = h.shape[0]
    parts = []
    for e in range(E_PER_G):
        sl = slice(e * D_EXPERT, (e + 1) * D_EXPERT)
        ae = a[:, sl]
        ce = jnp.broadcast_to(cw[:, e:e + 1], (tm, D_EXPERT))
        parts.append((ae * jax.nn.sigmoid(ae) * u[:, sl] * ce).astype(BF16))
    hid = jnp.concatenate(parts, axis=-1)
    acc_ref[...] += jnp.dot(hid, wd_ref[0], preferred_element_type=F32)

    @pl.when(g == N_GROUPS - 1)
    def _():
        gt2 = ada_ref[0, 5:6, :]
        x2 = x1_ref[...] + gt2 * acc_ref[...]
        if final_norm:
            x2 = _rms(x2, gf_ref[...])
        o_ref[...] = x2


def _moe_call(h2, comb, x1, ada_l, wg, wu, wd, gf, S, final_norm):
    T, D = x1.shape
    tm = TM_MOE
    tpb = S // tm
    EF = E_PER_G * D_EXPERT
    return pl.pallas_call(
        functools.partial(_moe_kernel, final_norm=final_norm),
        grid=(T // tm, N_GROUPS),
        in_specs=[
            pl.BlockSpec((tm, D), lambda i, g: (i, 0)),
            pl.BlockSpec((tm, LANE), lambda i, g: (i, g)),
            pl.BlockSpec((tm, D), lambda i, g: (i, 0)),
            pl.BlockSpec((1, 6, D), lambda i, g: (i // tpb, 0, 0)),
            pl.BlockSpec((1, D, EF), lambda i, g: (g, 0, 0)),
            pl.BlockSpec((1, D, EF), lambda i, g: (g, 0, 0)),
            pl.BlockSpec((1, EF, D), lambda i, g: (g, 0, 0)),
            pl.BlockSpec((1, D), lambda i, g: (0, 0)),
        ],
        out_specs=pl.BlockSpec((tm, D), lambda i, g: (i, 0)),
        out_shape=jax.ShapeDtypeStruct((T, D), F32),
        scratch_shapes=[pltpu.VMEM((tm, D), F32)],
        compiler_params=_cparams(("arbitrary", "arbitrary")),
        name="moe_experts",
    )(h2, comb, x1, ada_l, wg, wu, wd, gf)


def _pad_heads(w, width):
    k = w.shape[0]
    w = w.reshape(k, N_HEADS, width)
    w = jnp.pad(w, ((0, 0), (0, 0), (0, HEAD_PAD - width)))
    return w.reshape(k, N_HEADS * HEAD_PAD)


def _layout_w_in(w):
    d = w.shape[0]
    sp_kr = Q_LORA + KV_LORA
    sp_conv = sp_kr + QK_ROPE
    zeros = lambda n: jnp.zeros((d, n), w.dtype)
    return jnp.concatenate(
        [w[:, :sp_kr], zeros(KPE_OFF), w[:, sp_kr:sp_conv], zeros(LANE - KPE_OFF - QK_ROPE), w[:, sp_conv:]],
        axis=1)


def kernel(x, c, positions, ada_w, ada_b, norm1_g, norm2_g, w_in, q_norm_g, w_uq, kv_norm_g, w_ukv,
           w_o_attn, conv_w, conv_b, conv_ln_g, conv_ln_b, w_conv_out, w_out, router_group_w,
           router_group_b, router_expert_w, router_expert_b, expert_w_gate, expert_w_up,
           expert_w_down, final_norm_g):
    B, S, D = x.shape
    L = ada_w.shape[0]
    T = B * S

    c_pad = jnp.pad(c, ((0, 8 - B), (0, 0)))
    ada = _ada_call(c_pad, ada_w, ada_b)[:, :B].reshape(L, B, 6, D)

    inv_freq = ROPE_BASE ** (-jnp.arange(0, QK_ROPE, 2, dtype=F32) / QK_ROPE)
    freq_row = jnp.zeros((1, LANE), F32)
    freq_row = freq_row.at[0, KPE_OFF:KPE_OFF + QK_ROPE // 2].set(inv_freq)
    freq_row = freq_row.at[0, KPE_OFF + QK_ROPE // 2:KPE_OFF + QK_ROPE].set(inv_freq)
    tc, ts1, ts2 = _rope_call(positions.reshape(T, 1), freq_row)

    x2d = x.reshape(T, D)
    for l in range(L):
        w_in_p = _layout_w_in(w_in[l]).astype(BF16)
        w_uq_p = _pad_heads(w_uq[l], QK_DIM).astype(BF16)
        wkv = w_ukv[l].reshape(KV_LORA, N_HEADS, QK_NOPE + V_DIM)
        w_uk_p = _pad_heads(wkv[:, :, :QK_NOPE].reshape(KV_LORA, -1), QK_NOPE).astype(BF16)
        w_uv_p = _pad_heads(wkv[:, :, QK_NOPE:].reshape(KV_LORA, -1), V_DIM).astype(BF16)
        q, k, v, u, gates = _inproj_call(
            x2d, ada[l], norm1_g[l].reshape(1, D), w_in_p, q_norm_g[l].reshape(1, -1), w_uq_p,
            kv_norm_g[l].reshape(1, -1), w_uk_p, w_uv_p, tc, ts1, ts2, B, S)

        attn = _attn_call(q, k, v).reshape(T, N_HEADS * V_DIM)

        w_r = jnp.concatenate([router_group_w[l], router_expert_w[l]], axis=1)
        w_r = jnp.pad(w_r, ((0, 0), (0, LANE - w_r.shape[1]))).astype(BF16)
        b_r = jnp.concatenate([router_group_b[l], router_expert_b[l]])
        b_r = jnp.pad(b_r, (0, LANE - b_r.shape[0])).reshape(1, LANE)
        cw = jnp.pad(conv_w[l], ((0, HALO - CONV_K), (0, 0)))
        x1, h2, comb = _mix_call(
            attn, u, gates, x2d, ada[l], w_o_attn[l].astype(BF16), cw, conv_b[l].reshape(1, -1),
            conv_ln_g[l].reshape(1, -1), conv_ln_b[l].reshape(1, -1), w_conv_out[l].astype(BF16),
            w_out[l].astype(BF16), norm2_g[l].reshape(1, D), w_r, b_r, S)

        EF = E_PER_G * D_EXPERT
        wg = expert_w_gate[l].transpose(0, 2, 1, 3).reshape(N_GROUPS, D, EF).astype(BF16)
        wu = expert_w_up[l].transpose(0, 2, 1, 3).reshape(N_GROUPS, D, EF).astype(BF16)
        wd = expert_w_down[l].reshape(N_GROUPS, EF, D).astype(BF16)
        x2d = _moe_call(h2, comb, x1, ada[l], wg, wu, wd, final_norm_g.reshape(1, D), S,
                        final_norm=(l == L - 1))
    return x2d.reshape(B, S, D)
```

```python
import functools
import math

import jax
import jax.numpy as jnp
from jax import lax
from jax.experimental import pallas as pl
from jax.experimental.pallas import tpu as pltpu

D_MODEL = 1024
N_HEADS = 8
QK_NOPE = 64
QK_ROPE = 32
V_DIM = 64
QK_DIM = QK_NOPE + QK_ROPE
Q_LORA = 384
KV_LORA = 256
ROPE_BASE = 10000.0
CONV_C = 512
CONV_K = 31
N_GROUPS = 4
E_PER_G = 8
D_EXPERT = 128
EPS = 1e-6

LANE = 128
HEAD_PAD = LANE
KPE_OFF = QK_NOPE
VT_ROWS = 80
HALO = 32
VMEM_LIMIT = 56 * 1024 * 1024

C_QLAT = 0
C_KVLAT = Q_LORA
C_KPE = Q_LORA + KV_LORA
C_GLU_A = C_KPE + LANE
C_GLU_B = C_GLU_A + CONV_C
C_GATE = C_GLU_B + CONV_C
IN_COLS_PAD = C_GATE + 2 * D_MODEL

TM_IN = 512
TM_MIX = 512
TM_MOE = 512
TQ = 512
TK = 256

BF16 = jnp.bfloat16
F32 = jnp.float32


def _cparams(sem, flags=None):
    return pltpu.CompilerParams(dimension_semantics=sem, vmem_limit_bytes=VMEM_LIMIT, flags=flags)


def _ada_kernel(c_ref, w_ref, b_ref, o_ref):
    c = c_ref[...]
    c_act = c * jax.nn.sigmoid(c)
    o_ref[0] = jnp.dot(c_act, w_ref[0], preferred_element_type=F32,
                       precision=lax.Precision.HIGHEST) + b_ref[0]


def _ada_call(c_pad, ada_w, ada_b):
    L, D, N = ada_w.shape
    tn = 1536
    return pl.pallas_call(
        _ada_kernel,
        grid=(L, N // tn),
        in_specs=[
            pl.BlockSpec((8, D), lambda l, j: (0, 0)),
            pl.BlockSpec((1, D, tn), lambda l, j: (l, 0, j)),
            pl.BlockSpec((1, 1, tn), lambda l, j: (l, 0, j)),
        ],
        out_specs=pl.BlockSpec((1, 8, tn), lambda l, j: (l, 0, j)),
        out_shape=jax.ShapeDtypeStruct((L, 8, N), F32),
        compiler_params=_cparams(("arbitrary", "arbitrary")),
        name="ada_ln",
    )(c_pad, ada_w, ada_b.reshape(L, 1, N))


def _rope_kernel(pos_ref, freq_ref, c_ref, s1_ref, s2_ref):
    pos = pos_ref[...].astype(F32)
    ang = pos * freq_ref[...]
    lane = lax.broadcasted_iota(jnp.int32, ang.shape, 1)
    cosv = jnp.cos(ang)
    sinv = jnp.sin(ang)
    is_x1 = (lane >= KPE_OFF) & (lane < KPE_OFF + QK_ROPE // 2)
    is_x2 = (lane >= KPE_OFF + QK_ROPE // 2) & (lane < KPE_OFF + QK_ROPE)
    c_ref[...] = jnp.where(is_x1 | is_x2, cosv, jnp.where(lane < KPE_OFF, 1.0, 0.0))
    s1_ref[...] = jnp.where(is_x2, sinv, 0.0)
    s2_ref[...] = jnp.where(is_x1, -sinv, 0.0)


def _rope_call(pos_col, freq_row):
    T = pos_col.shape[0]
    tm = 2048
    spec = pl.BlockSpec((tm, LANE), lambda i: (i, 0))
    return pl.pallas_call(
        _rope_kernel,
        grid=(T // tm,),
        in_specs=[pl.BlockSpec((tm, 1), lambda i: (i, 0)),
                  pl.BlockSpec((1, LANE), lambda i: (0, 0))],
        out_specs=[spec, spec, spec],
        out_shape=[jax.ShapeDtypeStruct((T, LANE), F32)] * 3,
        compiler_params=_cparams(("arbitrary",)),
        name="rope_tables",
    )(pos_col, freq_row)


def _rms(x, g):
    ms = jnp.mean(x * x, axis=-1, keepdims=True)
    return x * lax.rsqrt(ms + EPS) * g


def _rope_apply(x, c, s1, s2):
    half = QK_ROPE // 2
    return x * c + pltpu.roll(x, half, axis=1) * s1 + pltpu.roll(x, LANE - half, axis=1) * s2


def _inproj_kernel(x_ref, ada_ref, g1_ref, win_ref, qg_ref, wuq_ref, kvg_ref, wuk_ref, wuvt_ref,
                   c_ref, s1_ref, s2_ref, q_ref, k_ref, vt_ref, u_ref, gate_ref):
    x = x_ref[...]
    sh1 = ada_ref[0, 0:1, :]
    sc1 = ada_ref[0, 1:2, :]
    h = _rms(x, g1_ref[...]) * (1.0 + sc1) + sh1
    z = jnp.dot(h.astype(BF16), win_ref[...], preferred_element_type=F32)

    qn = _rms(z[:, C_QLAT:C_QLAT + Q_LORA], qg_ref[...]).astype(BF16)
    kvn = _rms(z[:, C_KVLAT:C_KVLAT + KV_LORA], kvg_ref[...]).astype(BF16)
    q = jnp.dot(qn, wuq_ref[...], preferred_element_type=F32)
    kk = jnp.dot(kvn, wuk_ref[...], preferred_element_type=F32)
    vvt = lax.dot_general(wuvt_ref[...], kvn, (((1,), (1,)), ((), ())), preferred_element_type=F32)
    vrow = lax.broadcasted_iota(jnp.int32, (VT_ROWS, 1), 0)
    ones_row = jnp.where(vrow == V_DIM, 1.0, 0.0)

    c = c_ref[...]
    s1 = s1_ref[...]
    s2 = s2_ref[...]
    qs = (QK_DIM ** -0.5) * math.log2(math.e)
    cq, s1q, s2q = c * qs, s1 * qs, s2 * qs
    kpe = _rope_apply(z[:, C_KPE:C_KPE + LANE], c, s1, s2)
    for hh in range(N_HEADS):
        sl = slice(hh * HEAD_PAD, (hh + 1) * HEAD_PAD)
        q_ref[0, hh] = _rope_apply(q[:, sl], cq, s1q, s2q).astype(BF16)
        k_ref[0, hh] = (kk[:, sl] + kpe).astype(BF16)
        vth = (vvt[hh * VT_ROWS:(hh + 1) * VT_ROWS, :] + ones_row).astype(BF16)
        for j in range(vt_ref.shape[2]):
            vt_ref[0, hh, j] = vth[:, j * TK:(j + 1) * TK]

    u_ref[...] = (z[:, C_GLU_A:C_GLU_A + CONV_C]
                  * jax.nn.sigmoid(z[:, C_GLU_B:C_GLU_B + CONV_C])).astype(BF16)
    gate_ref[...] = jax.nn.sigmoid(z[:, C_GATE:C_GATE + 2 * D_MODEL]).astype(BF16)


def _inproj_call(x2d, ada_l, g1, w_in_p, qg, w_uq_p, kvg, w_uk_p, w_uvt_p, tc, ts1, ts2, B, S):
    T, D = x2d.shape
    tm = TM_IN
    tpb = S // tm
    kpt = tm // TK
    const = lambda shape: pl.BlockSpec(shape, lambda i: (0,) * len(shape))
    head_spec = pl.BlockSpec((1, N_HEADS, tm, HEAD_PAD), lambda i: (i // tpb, 0, i % tpb, 0))
    head_shape = jax.ShapeDtypeStruct((B, N_HEADS, S, HEAD_PAD), BF16)
    tab = pl.BlockSpec((tm, LANE), lambda i: (i, 0))
    return pl.pallas_call(
        _inproj_kernel,
        grid=(T // tm,),
        in_specs=[
            pl.BlockSpec((tm, D), lambda i: (i, 0)),
            pl.BlockSpec((1, 6, D), lambda i: (i // tpb, 0, 0)),
            const((1, D)),
            const((D, IN_COLS_PAD)),
            const((1, Q_LORA)),
            const((Q_LORA, N_HEADS * HEAD_PAD)),
            const((1, KV_LORA)),
            const((KV_LORA, N_HEADS * HEAD_PAD)),
            const((N_HEADS * VT_ROWS, KV_LORA)),
            tab, tab, tab,
        ],
        out_specs=[head_spec, head_spec,
                   pl.BlockSpec((1, N_HEADS, kpt, VT_ROWS, TK), lambda i: (i // tpb, 0, i % tpb, 0, 0)),
                   pl.BlockSpec((tm, CONV_C), lambda i: (i, 0)),
                   pl.BlockSpec((tm, 2 * D_MODEL), lambda i: (i, 0))],
        out_shape=[head_shape, head_shape,
                   jax.ShapeDtypeStruct((B, N_HEADS, S // TK, VT_ROWS, TK), BF16),
                   jax.ShapeDtypeStruct((T, CONV_C), BF16),
                   jax.ShapeDtypeStruct((T, 2 * D_MODEL), BF16)],
        compiler_params=_cparams(("arbitrary",)),
        name="in_proj",
    )(x2d, ada_l, g1, w_in_p, qg, w_uq_p, kvg, w_uk_p, w_uvt_p, tc, ts1, ts2)


HEADS_PER_STEP = 2


def _attn_kernel(q_ref, k_ref, vt_ref, o_ref, *scratch):
    qi = pl.program_id(2)
    heads = range(HEADS_PER_STEP)
    assert TQ == 2 * TK
    hs = HEADS_PER_STEP
    s_ref = [scratch[2 * hh:2 * hh + 2] for hh in heads]
    p_ref = [scratch[2 * hs + 2 * hh:2 * hs + 2 * hh + 2] for hh in heads]
    acc_ref = scratch[4 * hs:5 * hs]

    def qk(kb, slot):
        for hh in heads:
            k = k_ref[0, hh, pl.ds(kb * TK, TK), :]
            s_ref[hh][slot][...] = lax.dot_general(k, q_ref[0, hh], (((1,), (1,)), ((), ())),
                                              preferred_element_type=F32)

    def pv(kb, slot, alpha):
        for hh in heads:
            acc_ref[hh][...] = acc_ref[hh][...] * alpha[hh] + jnp.dot(vt_ref[0, hh, kb], p_ref[hh][slot][...],
                                                           preferred_element_type=F32)

    def softmax(slot, m, mask=None):
        m_out, alpha = [], []
        for hh in heads:
            s = s_ref[hh][slot][...]
            if mask is not None:
                s = jnp.where(mask, s, -jnp.inf)
            m_new = jnp.maximum(m[hh], jnp.max(s, axis=0, keepdims=True))
            p_ref[hh][slot][...] = jnp.exp2(s - m_new).astype(BF16)
            alpha.append(jnp.exp2(m[hh] - m_new))
            m_out.append(m_new)
        return tuple(m_out), tuple(alpha)

    for hh in heads:
        acc_ref[hh][...] = jnp.zeros((VT_ROWS, TQ), F32)
        p_ref[hh][1][...] = jnp.zeros((TK, TQ), BF16)
    m0 = tuple(jnp.full((1, TQ), -jnp.inf, F32) for _ in heads)
    one = tuple(jnp.ones((1, TQ), F32) for _ in heads)
    qk(0, 0)

    def body(i, carry):
        m, alpha1 = carry
        a = 2 * i
        qk(a + 1, 1)
        pv(jnp.maximum(a - 1, 0), 1, alpha1)
        m, alpha0 = softmax(0, m)
        qk(a + 2, 0)
        pv(a, 0, alpha0)
        m, alpha1 = softmax(1, m)
        return m, alpha1

    m, alpha1 = lax.fori_loop(0, qi, body, (m0, one))
    a = 2 * qi
    key = lax.broadcasted_iota(jnp.int32, (TK, TQ), 0)
    qry = lax.broadcasted_iota(jnp.int32, (TK, TQ), 1)
    qk(a + 1, 1)
    pv(jnp.maximum(a - 1, 0), 1, alpha1)
    m, alpha0 = softmax(0, m, key <= qry)
    pv(a, 0, alpha0)
    m, alpha1 = softmax(1, m, key + TK <= qry)
    pv(a + 1, 1, alpha1)
    outs = []
    for hh in heads:
        acc = acc_ref[hh][...]
        o_t = acc[:V_DIM, :] / acc[V_DIM:V_DIM + 1, :]
        outs.append(o_t.T)
    o_ref[0] = jnp.concatenate(outs, axis=-1).astype(BF16)


def _attn_call(q, k, vt):
    B, H, S, _ = q.shape
    nq = S // TQ
    hs = HEADS_PER_STEP
    return pl.pallas_call(
        _attn_kernel,
        grid=(B, H // hs, nq),
        in_specs=[
            pl.BlockSpec((1, hs, TQ, HEAD_PAD), lambda b, h, i: (b, h, i, 0)),
            pl.BlockSpec((1, hs, S, HEAD_PAD), lambda b, h, i: (b, h, 0, 0)),
            pl.BlockSpec((1, hs, S // TK, VT_ROWS, TK), lambda b, h, i: (b, h, 0, 0, 0)),
        ],
        out_specs=pl.BlockSpec((1, TQ, hs * V_DIM), lambda b, h, i: (b, i, h)),
        out_shape=jax.ShapeDtypeStruct((B, S, H * V_DIM), BF16),
        scratch_shapes=([pltpu.VMEM((TK, TQ), F32)] * (2 * hs) + [pltpu.VMEM((TK, TQ), BF16)] * (2 * hs)
                        + [pltpu.VMEM((VT_ROWS, TQ), F32)] * hs),
        compiler_params=_cparams(("arbitrary", "arbitrary", "arbitrary")),
        name="mla_attention",
    )(q, k, vt)


def _mix_kernel(attn_ref, u_ref, halo_ref, gate_ref, x_ref, ada_ref, woa_ref, cw_ref, cb_ref,
                lng_ref, lnb_ref, wco_ref, wout_ref, g2_ref, wr_ref, br_ref,
                x1_ref, h2_ref, comb_ref, buf_ref, *, tiles_per_batch):
    i = pl.program_id(0)
    tm = u_ref.shape[0]
    first = (i % tiles_per_batch) == 0

    buf_ref[0:HALO, :] = jnp.where(first, 0.0, halo_ref[...].astype(F32))
    buf_ref[HALO:HALO + tm, :] = u_ref[...].astype(F32)
    acc = jnp.zeros((tm, CONV_C), F32) + cb_ref[...]
    base = HALO - (CONV_K - 1)
    for j in range(CONV_K):
        acc = acc + buf_ref[pl.ds(base + j, tm), :] * cw_ref[j:j + 1, :]
    mu = jnp.mean(acc, axis=-1, keepdims=True)
    xc = acc - mu
    var = jnp.mean(xc * xc, axis=-1, keepdims=True)
    ln = xc * lax.rsqrt(var + EPS) * lng_ref[...] + lnb_ref[...]
    uc = ln * jax.nn.sigmoid(ln)
    y_conv = jnp.dot(uc.astype(BF16), wco_ref[...], preferred_element_type=F32)
    y_attn = jnp.dot(attn_ref[...], woa_ref[...], preferred_element_type=F32)

    g_attn = gate_ref[:, 0:D_MODEL].astype(F32)
    g_conv = gate_ref[:, D_MODEL:2 * D_MODEL].astype(F32)
    merged = g_attn * y_attn + g_conv * y_conv
    y = jnp.dot(merged.astype(BF16), wout_ref[...], preferred_element_type=F32)
    gt1 = ada_ref[0, 2:3, :]
    x1 = x_ref[...] + gt1 * y
    x1_ref[...] = x1

    sh2 = ada_ref[0, 3:4, :]
    sc2 = ada_ref[0, 4:5, :]
    h2 = (_rms(x1, g2_ref[...]) * (1.0 + sc2) + sh2).astype(BF16)
    h2_ref[...] = h2

    logits = jnp.dot(h2, wr_ref[...], preferred_element_type=F32) + br_ref[...]
    lane = lax.broadcasted_iota(jnp.int32, logits.shape, 1)
    big = jnp.int32(1 << 20)
    neg = -jnp.inf
    lg = jnp.where(lane < N_GROUPS, logits, neg)
    gmax = jnp.max(lg, axis=-1, keepdims=True)
    g_val = 1.0 / jnp.sum(jnp.exp(lg - gmax), axis=-1, keepdims=True)
    g_idx = jnp.min(jnp.where(lg == gmax, lane, big), axis=-1, keepdims=True)
    e_lo = N_GROUPS + g_idx * E_PER_G
    emask = (lane >= e_lo) & (lane < e_lo + E_PER_G)
    le = jnp.where(emask, logits, neg)
    m1 = jnp.max(le, axis=-1, keepdims=True)
    i1 = jnp.min(jnp.where(le == m1, lane, big), axis=-1, keepdims=True)
    le2 = jnp.where(lane == i1, neg, le)
    m2 = jnp.max(le2, axis=-1, keepdims=True)
    i2 = jnp.min(jnp.where(le2 == m2, lane, big), axis=-1, keepdims=True)
    z_e = jnp.sum(jnp.exp(le - m1), axis=-1, keepdims=True)
    p1 = 1.0 / z_e
    p2 = jnp.exp(m2 - m1) / z_e
    w1 = g_val * (p1 / (p1 + p2))
    w2 = g_val * (p2 / (p1 + p2))
    comb = jnp.where(lane == i1, w1, 0.0) + jnp.where(lane == i2, w2, 0.0)
    for g in range(N_GROUPS):
        shifted = pltpu.roll(comb, LANE - (N_GROUPS + g * E_PER_G), axis=1)
        comb_ref[:, g * LANE:(g + 1) * LANE] = jnp.where(lane < E_PER_G, shifted, 0.0)


def _mix_call(attn2d, u2d, gates, x2d, ada_l, woa, cw, cb, lng, lnb, wco, wout, g2, wr, br, S):
    T, D = x2d.shape
    tm = TM_MIX
    tpb = S // tm
    hb = tm // HALO
    const = lambda shape: pl.BlockSpec(shape, lambda i: (0,) * len(shape))
    row = lambda w: pl.BlockSpec((tm, w), lambda i: (i, 0))
    return pl.pallas_call(
        functools.partial(_mix_kernel, tiles_per_batch=tpb),
        grid=(T // tm,),
        in_specs=[
            row(N_HEADS * V_DIM),
            row(CONV_C),
            pl.BlockSpec((HALO, CONV_C), lambda i: (jnp.maximum(i * hb - 1, 0), 0)),
            row(2 * D),
            row(D),
            pl.BlockSpec((1, 6, D), lambda i: (i // tpb, 0, 0)),
            const((N_HEADS * V_DIM, D)),
            const((HALO, CONV_C)),
            const((1, CONV_C)),
            const((1, CONV_C)),
            const((1, CONV_C)),
            const((CONV_C, D)),
            const((D, D)),
            const((1, D)),
            const((D, LANE)),
            const((1, LANE)),
        ],
        out_specs=[row(D), row(D), row(N_GROUPS * LANE)],
        out_shape=[jax.ShapeDtypeStruct((T, D), F32),
                   jax.ShapeDtypeStruct((T, D), BF16),
                   jax.ShapeDtypeStruct((T, N_GROUPS * LANE), F32)],
        scratch_shapes=[pltpu.VMEM((HALO + tm, CONV_C), F32)],
        compiler_params=_cparams(("arbitrary",)),
        name="mixer_merge",
    )(attn2d, u2d, u2d, gates, x2d, ada_l, woa, cw, cb, lng, lnb, wco, wout, g2, wr, br)


def _moe_kernel(h_ref, comb_ref, x1_ref, ada_ref, wg_ref, wu_ref, wd_ref, gf_ref, o_ref, acc_ref,
                *, final_norm):
    g = pl.program_id(1)

    @pl.when(g == 0)
    def _():
        acc_ref[...] = jnp.zeros_like(acc_ref)

    h = h_ref[...]
    a = jnp.dot(h, wg_ref[0], preferred_element_type=F32)
    u = jnp.dot(h, wu_ref[0], preferred_element_type=F32)
    cw = comb_ref[...]
    tm = h.shape[0]
    parts = []
    for e in range(E_PER_G):
        sl = slice(e * D_EXPERT, (e + 1) * D_EXPERT)
        ae = a[:, sl]
        ce = jnp.broadcast_to(cw[:, e:e + 1], (tm, D_EXPERT))
        parts.append((ae * jax.nn.sigmoid(ae) * u[:, sl] * ce).astype(BF16))
    hid = jnp.concatenate(parts, axis=-1)
    acc_ref[...] += jnp.dot(hid, wd_ref[0], preferred_element_type=F32)

    @pl.when(g == N_GROUPS - 1)
    def _():
        gt2 = ada_ref[0, 5:6, :]
        x2 = x1_ref[...] + gt2 * acc_ref[...]
        if final_norm:
            x2 = _rms(x2, gf_ref[...])
        o_ref[...] = x2


def _moe_call(h2, comb, x1, ada_l, wg, wu, wd, gf, S, final_norm):
    T, D = x1.shape
    tm = TM_MOE
    tpb = S // tm
    EF = E_PER_G * D_EXPERT
    return pl.pallas_call(
        functools.partial(_moe_kernel, final_norm=final_norm),
        grid=(T // tm, N_GROUPS),
        in_specs=[
            pl.BlockSpec((tm, D), lambda i, g: (i, 0)),
            pl.BlockSpec((tm, LANE), lambda i, g: (i, g)),
            pl.BlockSpec((tm, D), lambda i, g: (i, 0)),
            pl.BlockSpec((1, 6, D), lambda i, g: (i // tpb, 0, 0)),
            pl.BlockSpec((1, D, EF), lambda i, g: (g, 0, 0)),
            pl.BlockSpec((1, D, EF), lambda i, g: (g, 0, 0)),
            pl.BlockSpec((1, EF, D), lambda i, g: (g, 0, 0)),
            pl.BlockSpec((1, D), lambda i, g: (0, 0)),
        ],
        out_specs=pl.BlockSpec((tm, D), lambda i, g: (i, 0)),
        out_shape=jax.ShapeDtypeStruct((T, D), F32),
        scratch_shapes=[pltpu.VMEM((tm, D), F32)],
        compiler_params=_cparams(("arbitrary", "arbitrary")),
        name="moe_experts",
    )(h2, comb, x1, ada_l, wg, wu, wd, gf)


def _pad_heads(w, width):
    k = w.shape[0]
    w = w.reshape(k, N_HEADS, width)
    w = jnp.pad(w, ((0, 0), (0, 0), (0, HEAD_PAD - width)))
    return w.reshape(k, N_HEADS * HEAD_PAD)


def _layout_w_in(w):
    d = w.shape[0]
    sp_kr = Q_LORA + KV_LORA
    sp_conv = sp_kr + QK_ROPE
    zeros = lambda n: jnp.zeros((d, n), w.dtype)
    return jnp.concatenate(
        [w[:, :sp_kr], zeros(KPE_OFF), w[:, sp_kr:sp_conv], zeros(LANE - KPE_OFF - QK_ROPE), w[:, sp_conv:]],
        axis=1)


def kernel(x, c, positions, ada_w, ada_b, norm1_g, norm2_g, w_in, q_norm_g, w_uq, kv_norm_g, w_ukv,
           w_o_attn, conv_w, conv_b, conv_ln_g, conv_ln_b, w_conv_out, w_out, router_group_w,
           router_group_b, router_expert_w, router_expert_b, expert_w_gate, expert_w_up,
           expert_w_down, final_norm_g):
    B, S, D = x.shape
    L = ada_w.shape[0]
    T = B * S

    c_pad = jnp.pad(c, ((0, 8 - B), (0, 0)))
    ada = _ada_call(c_pad, ada_w, ada_b)[:, :B].reshape(L, B, 6, D)

    inv_freq = ROPE_BASE ** (-jnp.arange(0, QK_ROPE, 2, dtype=F32) / QK_ROPE)
    freq_row = jnp.zeros((1, LANE), F32)
    freq_row = freq_row.at[0, KPE_OFF:KPE_OFF + QK_ROPE // 2].set(inv_freq)
    freq_row = freq_row.at[0, KPE_OFF + QK_ROPE // 2:KPE_OFF + QK_ROPE].set(inv_freq)
    tc, ts1, ts2 = _rope_call(positions.reshape(T, 1), freq_row)

    x2d = x.reshape(T, D)
    for l in range(L):
        w_in_p = _layout_w_in(w_in[l]).astype(BF16)
        w_uq_p = _pad_heads(w_uq[l], QK_DIM).astype(BF16)
        wkv = w_ukv[l].reshape(KV_LORA, N_HEADS, QK_NOPE + V_DIM)
        w_uk_p = _pad_heads(wkv[:, :, :QK_NOPE].reshape(KV_LORA, -1), QK_NOPE).astype(BF16)
        w_uvt = jnp.pad(wkv[:, :, QK_NOPE:], ((0, 0), (0, 0), (0, VT_ROWS - V_DIM)))
        w_uvt_p = w_uvt.reshape(KV_LORA, N_HEADS * VT_ROWS).T.astype(BF16)
        q, k, vt, u, gates = _inproj_call(
            x2d, ada[l], norm1_g[l].reshape(1, D), w_in_p, q_norm_g[l].reshape(1, -1), w_uq_p,
            kv_norm_g[l].reshape(1, -1), w_uk_p, w_uvt_p, tc, ts1, ts2, B, S)

        attn = _attn_call(q, k, vt).reshape(T, N_HEADS * V_DIM)

        w_r = jnp.concatenate([router_group_w[l], router_expert_w[l]], axis=1)
        w_r = jnp.pad(w_r, ((0, 0), (0, LANE - w_r.shape[1]))).astype(BF16)
        b_r = jnp.concatenate([router_group_b[l], router_expert_b[l]])
        b_r = jnp.pad(b_r, (0, LANE - b_r.shape[0])).reshape(1, LANE)
        cw = jnp.pad(conv_w[l], ((0, HALO - CONV_K), (0, 0)))
        x1, h2, comb = _mix_call(
            attn, u, gates, x2d, ada[l], w_o_attn[l].astype(BF16), cw, conv_b[l].reshape(1, -1),
            conv_ln_g[l].reshape(1, -1), conv_ln_b[l].reshape(1, -1), w_conv_out[l].astype(BF16),
            w_out[l].astype(BF16), norm2_g[l].reshape(1, D), w_r, b_r, S)

        EF = E_PER_G * D_EXPERT
        wg = expert_w_gate[l].transpose(0, 2, 1, 3).reshape(N_GROUPS, D, EF).astype(BF16)
        wu = expert_w_up[l].transpose(0, 2, 1, 3).reshape(N_GROUPS, D, EF).astype(BF16)
        wd = expert_w_down[l].reshape(N_GROUPS, EF, D).astype(BF16)
        x2d = _moe_call(h2, comb, x1, ada[l], wg, wu, wd, final_norm_g.reshape(1, D), S,
                        final_norm=(l == L - 1))
    return x2d.reshape(B, S, D)
```

```python
import functools
import math

import jax
import jax.numpy as jnp
from jax import lax
from jax.experimental import pallas as pl
from jax.experimental.pallas import tpu as pltpu

D_MODEL = 1024
N_HEADS = 8
QK_NOPE = 64
QK_ROPE = 32
V_DIM = 64
QK_DIM = QK_NOPE + QK_ROPE
Q_LORA = 384
KV_LORA = 256
ROPE_BASE = 10000.0
CONV_C = 512
CONV_K = 31
N_GROUPS = 4
E_PER_G = 8
D_EXPERT = 128
EPS = 1e-6

LANE = 128
SUBLANE = 8
CONV_TCHUNK = 128
HEAD_PAD = LANE
KPE_OFF = QK_NOPE
VT_ROWS = 80
HALO = 32
VMEM_LIMIT = 56 * 1024 * 1024

C_QLAT = 0
C_KVLAT = Q_LORA
C_KPE = Q_LORA + KV_LORA
C_GLU_A = C_KPE + LANE
C_GLU_B = C_GLU_A + CONV_C
C_GATE = C_GLU_B + CONV_C
IN_COLS_PAD = C_GATE + 2 * D_MODEL

TM_IN = 512
TM_MIX = 512
TM_MOE = 512
TQ = 512
TK = 256

BF16 = jnp.bfloat16
F32 = jnp.float32


def _cparams(sem, flags=None):
    return pltpu.CompilerParams(dimension_semantics=sem, vmem_limit_bytes=VMEM_LIMIT, flags=flags)


def _ada_kernel(c_ref, w_ref, b_ref, o_ref):
    c = c_ref[...]
    c_act = c * jax.nn.sigmoid(c)
    o_ref[0] = jnp.dot(c_act, w_ref[0], preferred_element_type=F32,
                       precision=lax.Precision.HIGHEST) + b_ref[0]


def _ada_call(c_pad, ada_w, ada_b):
    L, D, N = ada_w.shape
    tn = 1536
    return pl.pallas_call(
        _ada_kernel,
        grid=(L, N // tn),
        in_specs=[
            pl.BlockSpec((8, D), lambda l, j: (0, 0)),
            pl.BlockSpec((1, D, tn), lambda l, j: (l, 0, j)),
            pl.BlockSpec((1, 1, tn), lambda l, j: (l, 0, j)),
        ],
        out_specs=pl.BlockSpec((1, 8, tn), lambda l, j: (l, 0, j)),
        out_shape=jax.ShapeDtypeStruct((L, 8, N), F32),
        compiler_params=_cparams(("arbitrary", "arbitrary")),
        name="ada_ln",
    )(c_pad, ada_w, ada_b.reshape(L, 1, N))


def _rope_kernel(pos_ref, freq_ref, c_ref, s1_ref, s2_ref):
    pos = pos_ref[...].astype(F32)
    ang = pos * freq_ref[...]
    lane = lax.broadcasted_iota(jnp.int32, ang.shape, 1)
    cosv = jnp.cos(ang)
    sinv = jnp.sin(ang)
    is_x1 = (lane >= KPE_OFF) & (lane < KPE_OFF + QK_ROPE // 2)
    is_x2 = (lane >= KPE_OFF + QK_ROPE // 2) & (lane < KPE_OFF + QK_ROPE)
    c_ref[...] = jnp.where(is_x1 | is_x2, cosv, jnp.where(lane < KPE_OFF, 1.0, 0.0))
    s1_ref[...] = jnp.where(is_x2, sinv, 0.0)
    s2_ref[...] = jnp.where(is_x1, -sinv, 0.0)


def _rope_call(pos_col, freq_row):
    T = pos_col.shape[0]
    tm = 2048
    spec = pl.BlockSpec((tm, LANE), lambda i: (i, 0))
    return pl.pallas_call(
        _rope_kernel,
        grid=(T // tm,),
        in_specs=[pl.BlockSpec((tm, 1), lambda i: (i, 0)),
                  pl.BlockSpec((1, LANE), lambda i: (0, 0))],
        out_specs=[spec, spec, spec],
        out_shape=[jax.ShapeDtypeStruct((T, LANE), F32)] * 3,
        compiler_params=_cparams(("arbitrary",)),
        name="rope_tables",
    )(pos_col, freq_row)


def _rms(x, g):
    ms = jnp.mean(x * x, axis=-1, keepdims=True)
    return x * lax.rsqrt(ms + EPS) * g


def _rope_apply(x, c, s1, s2):
    half = QK_ROPE // 2
    return x * c + pltpu.roll(x, half, axis=1) * s1 + pltpu.roll(x, LANE - half, axis=1) * s2


def _inproj_kernel(x_ref, ada_ref, g1_ref, win_ref, qg_ref, wuq_ref, kvg_ref, wuk_ref, wuvt_ref,
                   c_ref, s1_ref, s2_ref, q_ref, k_ref, vt_ref, u_ref, gate_ref):
    x = x_ref[...]
    sh1 = ada_ref[0, 0:1, :]
    sc1 = ada_ref[0, 1:2, :]
    h = _rms(x, g1_ref[...]) * (1.0 + sc1) + sh1
    z = jnp.dot(h.astype(BF16), win_ref[...], preferred_element_type=F32)

    qn = _rms(z[:, C_QLAT:C_QLAT + Q_LORA], qg_ref[...]).astype(BF16)
    kvn = _rms(z[:, C_KVLAT:C_KVLAT + KV_LORA], kvg_ref[...]).astype(BF16)
    q = jnp.dot(qn, wuq_ref[...], preferred_element_type=F32)
    kk = jnp.dot(kvn, wuk_ref[...], preferred_element_type=F32)
    vvt = lax.dot_general(wuvt_ref[...], kvn, (((1,), (1,)), ((), ())), preferred_element_type=F32)
    vrow = lax.broadcasted_iota(jnp.int32, (VT_ROWS, 1), 0)
    ones_row = jnp.where(vrow == V_DIM, 1.0, 0.0)

    c = c_ref[...]
    s1 = s1_ref[...]
    s2 = s2_ref[...]
    qs = (QK_DIM ** -0.5) * math.log2(math.e)
    cq, s1q, s2q = c * qs, s1 * qs, s2 * qs
    kpe = _rope_apply(z[:, C_KPE:C_KPE + LANE], c, s1, s2)
    for hh in range(N_HEADS):
        sl = slice(hh * HEAD_PAD, (hh + 1) * HEAD_PAD)
        q_ref[0, hh] = _rope_apply(q[:, sl], cq, s1q, s2q).astype(BF16)
        k_ref[0, hh] = (kk[:, sl] + kpe).astype(BF16)
        vth = (vvt[hh * VT_ROWS:(hh + 1) * VT_ROWS, :] + ones_row).astype(BF16)
        for j in range(vt_ref.shape[2]):
            vt_ref[0, hh, j] = vth[:, j * TK:(j + 1) * TK]

    u_ref[...] = (z[:, C_GLU_A:C_GLU_A + CONV_C]
                  * jax.nn.sigmoid(z[:, C_GLU_B:C_GLU_B + CONV_C])).astype(BF16)
    gate_ref[...] = jax.nn.sigmoid(z[:, C_GATE:C_GATE + 2 * D_MODEL]).astype(BF16)


def _inproj_call(x2d, ada_l, g1, w_in_p, qg, w_uq_p, kvg, w_uk_p, w_uvt_p, tc, ts1, ts2, B, S):
    T, D = x2d.shape
    tm = TM_IN
    tpb = S // tm
    kpt = tm // TK
    const = lambda shape: pl.BlockSpec(shape, lambda i: (0,) * len(shape))
    head_spec = pl.BlockSpec((1, N_HEADS, tm, HEAD_PAD), lambda i: (i // tpb, 0, i % tpb, 0))
    head_shape = jax.ShapeDtypeStruct((B, N_HEADS, S, HEAD_PAD), BF16)
    tab = pl.BlockSpec((tm, LANE), lambda i: (i, 0))
    return pl.pallas_call(
        _inproj_kernel,
        grid=(T // tm,),
        in_specs=[
            pl.BlockSpec((tm, D), lambda i: (i, 0)),
            pl.BlockSpec((1, 6, D), lambda i: (i // tpb, 0, 0)),
            const((1, D)),
            const((D, IN_COLS_PAD)),
            const((1, Q_LORA)),
            const((Q_LORA, N_HEADS * HEAD_PAD)),
            const((1, KV_LORA)),
            const((KV_LORA, N_HEADS * HEAD_PAD)),
            const((N_HEADS * VT_ROWS, KV_LORA)),
            tab, tab, tab,
        ],
        out_specs=[head_spec, head_spec,
                   pl.BlockSpec((1, N_HEADS, kpt, VT_ROWS, TK), lambda i: (i // tpb, 0, i % tpb, 0, 0)),
                   pl.BlockSpec((tm, CONV_C), lambda i: (i, 0)),
                   pl.BlockSpec((tm, 2 * D_MODEL), lambda i: (i, 0))],
        out_shape=[head_shape, head_shape,
                   jax.ShapeDtypeStruct((B, N_HEADS, S // TK, VT_ROWS, TK), BF16),
                   jax.ShapeDtypeStruct((T, CONV_C), BF16),
                   jax.ShapeDtypeStruct((T, 2 * D_MODEL), BF16)],
        compiler_params=_cparams(("arbitrary",)),
        name="in_proj",
    )(x2d, ada_l, g1, w_in_p, qg, w_uq_p, kvg, w_uk_p, w_uvt_p, tc, ts1, ts2)


HEADS_PER_STEP = 4


def _attn_kernel(q_ref, k_ref, vt_ref, o_ref, *scratch):
    qi = pl.program_id(2)
    heads = range(HEADS_PER_STEP)
    assert TQ == 2 * TK
    hs = HEADS_PER_STEP
    s_ref = [scratch[2 * hh:2 * hh + 2] for hh in heads]
    p_ref = [scratch[2 * hs + 2 * hh:2 * hs + 2 * hh + 2] for hh in heads]
    acc_ref = scratch[4 * hs:5 * hs]

    def qk(kb, slot):
        for hh in heads:
            k = k_ref[0, hh, pl.ds(kb * TK, TK), :]
            s_ref[hh][slot][...] = lax.dot_general(k, q_ref[0, hh], (((1,), (1,)), ((), ())),
                                              preferred_element_type=F32)

    def pv(kb, slot, alpha):
        for hh in heads:
            acc_ref[hh][...] = acc_ref[hh][...] * alpha[hh] + jnp.dot(vt_ref[0, hh, kb], p_ref[hh][slot][...],
                                                           preferred_element_type=F32)

    def softmax(slot, m, mask=None):
        m_out, alpha = [], []
        for hh in heads:
            s = s_ref[hh][slot][...]
            if mask is not None:
                s = jnp.where(mask, s, -jnp.inf)
            m_new = jnp.maximum(m[hh], jnp.max(s, axis=0, keepdims=True))
            p_ref[hh][slot][...] = jnp.exp2(s - m_new).astype(BF16)
            alpha.append(jnp.exp2(m[hh] - m_new))
            m_out.append(m_new)
        return tuple(m_out), tuple(alpha)

    for hh in heads:
        acc_ref[hh][...] = jnp.zeros((VT_ROWS, TQ), F32)
        p_ref[hh][1][...] = jnp.zeros((TK, TQ), BF16)
    m0 = tuple(jnp.full((1, TQ), -jnp.inf, F32) for _ in heads)
    one = tuple(jnp.ones((1, TQ), F32) for _ in heads)
    qk(0, 0)

    def body(i, carry):
        m, alpha1 = carry
        a = 2 * i
        qk(a + 1, 1)
        pv(jnp.maximum(a - 1, 0), 1, alpha1)
        m, alpha0 = softmax(0, m)
        qk(a + 2, 0)
        pv(a, 0, alpha0)
        m, alpha1 = softmax(1, m)
        return m, alpha1

    m, alpha1 = lax.fori_loop(0, qi, body, (m0, one))
    a = 2 * qi
    key = lax.broadcasted_iota(jnp.int32, (TK, TQ), 0)
    qry = lax.broadcasted_iota(jnp.int32, (TK, TQ), 1)
    qk(a + 1, 1)
    pv(jnp.maximum(a - 1, 0), 1, alpha1)
    m, alpha0 = softmax(0, m, key <= qry)
    pv(a, 0, alpha0)
    m, alpha1 = softmax(1, m, key + TK <= qry)
    pv(a + 1, 1, alpha1)
    outs = []
    for hh in heads:
        acc = acc_ref[hh][...]
        o_t = acc[:V_DIM, :] / acc[V_DIM:V_DIM + 1, :]
        outs.append(o_t.T)
    o_ref[0] = jnp.concatenate(outs, axis=-1).astype(BF16)


def _attn_call(q, k, vt):
    B, H, S, _ = q.shape
    nq = S // TQ
    hs = HEADS_PER_STEP
    return pl.pallas_call(
        _attn_kernel,
        grid=(B, H // hs, nq),
        in_specs=[
            pl.BlockSpec((1, hs, TQ, HEAD_PAD), lambda b, h, i: (b, h, i, 0)),
            pl.BlockSpec((1, hs, S, HEAD_PAD), lambda b, h, i: (b, h, 0, 0)),
            pl.BlockSpec((1, hs, S // TK, VT_ROWS, TK), lambda b, h, i: (b, h, 0, 0, 0)),
        ],
        out_specs=pl.BlockSpec((1, TQ, hs * V_DIM), lambda b, h, i: (b, i, h)),
        out_shape=jax.ShapeDtypeStruct((B, S, H * V_DIM), BF16),
        scratch_shapes=([pltpu.VMEM((TK, TQ), F32)] * (2 * hs) + [pltpu.VMEM((TK, TQ), BF16)] * (2 * hs)
                        + [pltpu.VMEM((VT_ROWS, TQ), F32)] * hs),
        compiler_params=_cparams(("arbitrary", "arbitrary", "arbitrary")),
        name="mla_attention",
    )(q, k, vt)


def _mix_kernel(attn_ref, u_ref, halo_ref, gate_ref, x_ref, ada_ref, woa_ref, cw_ref, cb_ref,
                lng_ref, lnb_ref, wco_ref, wout_ref, g2_ref, wr_ref, br_ref,
                x1_ref, h2_ref, comb_ref, buf_ref, *, tiles_per_batch):
    i = pl.program_id(0)
    tm = u_ref.shape[0]
    first = (i % tiles_per_batch) == 0

    buf_ref[0:HALO, :] = jnp.where(first, 0.0, halo_ref[...].astype(F32))
    buf_ref[HALO:HALO + tm, :] = u_ref[...].astype(F32)
    base = HALO - (CONV_K - 1)
    wrows = CONV_TCHUNK + HALO
    pieces = []
    for t0 in range(0, tm, CONV_TCHUNK):
        row = []
        for c0 in range(0, CONV_C, LANE):
            cs = slice(c0, c0 + LANE)
            win = buf_ref[t0:t0 + wrows, cs]
            acc = jnp.broadcast_to(cb_ref[:, cs], (CONV_TCHUNK, LANE))
            for r in range(SUBLANE):
                shifted = win if r == 0 else pltpu.roll(win, wrows - r, axis=0)
                for o in range(base, base + CONV_K):
                    if o % SUBLANE == r:
                        acc = acc + shifted[o - r:o - r + CONV_TCHUNK] * cw_ref[o - base:o - base + 1, cs]
            row.append(acc)
        pieces.append(jnp.concatenate(row, axis=1))
    acc = jnp.concatenate(pieces, axis=0)
    mu = jnp.mean(acc, axis=-1, keepdims=True)
    xc = acc - mu
    var = jnp.mean(xc * xc, axis=-1, keepdims=True)
    ln = xc * lax.rsqrt(var + EPS) * lng_ref[...] + lnb_ref[...]
    uc = ln * jax.nn.sigmoid(ln)
    y_conv = jnp.dot(uc.astype(BF16), wco_ref[...], preferred_element_type=F32)
    y_attn = jnp.dot(attn_ref[...], woa_ref[...], preferred_element_type=F32)

    g_attn = gate_ref[:, 0:D_MODEL].astype(F32)
    g_conv = gate_ref[:, D_MODEL:2 * D_MODEL].astype(F32)
    merged = g_attn * y_attn + g_conv * y_conv
    y = jnp.dot(merged.astype(BF16), wout_ref[...], preferred_element_type=F32)
    gt1 = ada_ref[0, 2:3, :]
    x1 = x_ref[...] + gt1 * y
    x1_ref[...] = x1

    sh2 = ada_ref[0, 3:4, :]
    sc2 = ada_ref[0, 4:5, :]
    h2 = (_rms(x1, g2_ref[...]) * (1.0 + sc2) + sh2).astype(BF16)
    h2_ref[...] = h2

    logits = jnp.dot(h2, wr_ref[...], preferred_element_type=F32) + br_ref[...]
    lane = lax.broadcasted_iota(jnp.int32, logits.shape, 1)
    big = jnp.int32(1 << 20)
    neg = -jnp.inf
    lg = jnp.where(lane < N_GROUPS, logits, neg)
    gmax = jnp.max(lg, axis=-1, keepdims=True)
    g_val = 1.0 / jnp.sum(jnp.exp(lg - gmax), axis=-1, keepdims=True)
    g_idx = jnp.min(jnp.where(lg == gmax, lane, big), axis=-1, keepdims=True)
    e_lo = N_GROUPS + g_idx * E_PER_G
    emask = (lane >= e_lo) & (lane < e_lo + E_PER_G)
    le = jnp.where(emask, logits, neg)
    m1 = jnp.max(le, axis=-1, keepdims=True)
    i1 = jnp.min(jnp.where(le == m1, lane, big), axis=-1, keepdims=True)
    le2 = jnp.where(lane == i1, neg, le)
    m2 = jnp.max(le2, axis=-1, keepdims=True)
    i2 = jnp.min(jnp.where(le2 == m2, lane, big), axis=-1, keepdims=True)
    z_e = jnp.sum(jnp.exp(le - m1), axis=-1, keepdims=True)
    p1 = 1.0 / z_e
    p2 = jnp.exp(m2 - m1) / z_e
    w1 = g_val * (p1 / (p1 + p2))
    w2 = g_val * (p2 / (p1 + p2))
    comb = jnp.where(lane == i1, w1, 0.0) + jnp.where(lane == i2, w2, 0.0)
    for g in range(N_GROUPS):
        shifted = pltpu.roll(comb, LANE - (N_GROUPS + g * E_PER_G), axis=1)
        comb_ref[:, g * LANE:(g + 1) * LANE] = jnp.where(lane < E_PER_G, shifted, 0.0)


def _mix_call(attn2d, u2d, gates, x2d, ada_l, woa, cw, cb, lng, lnb, wco, wout, g2, wr, br, S):
    T, D = x2d.shape
    tm = TM_MIX
    tpb = S // tm
    hb = tm // HALO
    const = lambda shape: pl.BlockSpec(shape, lambda i: (0,) * len(shape))
    row = lambda w: pl.BlockSpec((tm, w), lambda i: (i, 0))
    return pl.pallas_call(
        functools.partial(_mix_kernel, tiles_per_batch=tpb),
        grid=(T // tm,),
        in_specs=[
            row(N_HEADS * V_DIM),
            row(CONV_C),
            pl.BlockSpec((HALO, CONV_C), lambda i: (jnp.maximum(i * hb - 1, 0), 0)),
            row(2 * D),
            row(D),
            pl.BlockSpec((1, 6, D), lambda i: (i // tpb, 0, 0)),
            const((N_HEADS * V_DIM, D)),
            const((HALO, CONV_C)),
            const((1, CONV_C)),
            const((1, CONV_C)),
            const((1, CONV_C)),
            const((CONV_C, D)),
            const((D, D)),
            const((1, D)),
            const((D, LANE)),
            const((1, LANE)),
        ],
        out_specs=[row(D), row(D), row(N_GROUPS * LANE)],
        out_shape=[jax.ShapeDtypeStruct((T, D), F32),
                   jax.ShapeDtypeStruct((T, D), BF16),
                   jax.ShapeDtypeStruct((T, N_GROUPS * LANE), F32)],
        scratch_shapes=[pltpu.VMEM((HALO + tm, CONV_C), F32)],
        compiler_params=_cparams(("arbitrary",)),
        name="mixer_merge",
    )(attn2d, u2d, u2d, gates, x2d, ada_l, woa, cw, cb, lng, lnb, wco, wout, g2, wr, br)


def _moe_kernel(h_ref, comb_ref, x1_ref, ada_ref, wg_ref, wu_ref, wd_ref, gf_ref, o_ref, acc_ref,
                *, final_norm):
    g = pl.program_id(1)

    @pl.when(g == 0)
    def _():
        acc_ref[...] = jnp.zeros_like(acc_ref)

    h = h_ref[...]
    a = jnp.dot(h, wg_ref[0], preferred_element_type=F32)
    u = jnp.dot(h, wu_ref[0], preferred_element_type=F32)
    cw = comb_ref[...]
    tm = h.shape[0]
    parts = []
    for e in range(E_PER_G):
        sl = slice(e * D_EXPERT, (e + 1) * D_EXPERT)
        ae = a[:, sl]
        ce = jnp.broadcast_to(cw[:, e:e + 1], (tm, D_EXPERT))
        parts.append((ae * jax.nn.sigmoid(ae) * u[:, sl] * ce).astype(BF16))
    hid = jnp.concatenate(parts, axis=-1)
    acc_ref[...] += jnp.dot(hid, wd_ref[0], preferred_element_type=F32)

    @pl.when(g == N_GROUPS - 1)
    def _():
        gt2 = ada_ref[0, 5:6, :]
        x2 = x1_ref[...] + gt2 * acc_ref[...]
        if final_norm:
            x2 = _rms(x2, gf_ref[...])
        o_ref[...] = x2


def _moe_call(h2, comb, x1, ada_l, wg, wu, wd, gf, S, final_norm):
    T, D = x1.shape
    tm = TM_MOE
    tpb = S // tm
    EF = E_PER_G * D_EXPERT
    return pl.pallas_call(
        functools.partial(_moe_kernel, final_norm=final_norm),
        grid=(T // tm, N_GROUPS),
        in_specs=[
            pl.BlockSpec((tm, D), lambda i, g: (i, 0)),
            pl.BlockSpec((tm, LANE), lambda i, g: (i, g)),
            pl.BlockSpec((tm, D), lambda i, g: (i, 0)),
            pl.BlockSpec((1, 6, D), lambda i, g: (i // tpb, 0, 0)),
            pl.BlockSpec((1, D, EF), lambda i, g: (g, 0, 0)),
            pl.BlockSpec((1, D, EF), lambda i, g: (g, 0, 0)),
            pl.BlockSpec((1, EF, D), lambda i, g: (g, 0, 0)),
            pl.BlockSpec((1, D), lambda i, g: (0, 0)),
        ],
        out_specs=pl.BlockSpec((tm, D), lambda i, g: (i, 0)),
        out_shape=jax.ShapeDtypeStruct((T, D), F32),
        scratch_shapes=[pltpu.VMEM((tm, D), F32)],
        compiler_params=_cparams(("arbitrary", "arbitrary")),
        name="moe_experts",
    )(h2, comb, x1, ada_l, wg, wu, wd, gf)


def _pad_heads(w, width):
    k = w.shape[0]
    w = w.reshape(k, N_HEADS, width)
    w = jnp.pad(w, ((0, 0), (0, 0), (0, HEAD_PAD - width)))
    return w.reshape(k, N_HEADS * HEAD_PAD)


def _layout_w_in(w):
    d = w.shape[0]
    sp_kr = Q_LORA + KV_LORA
    sp_conv = sp_kr + QK_ROPE
    zeros = lambda n: jnp.zeros((d, n), w.dtype)
    return jnp.concatenate(
        [w[:, :sp_kr], zeros(KPE_OFF), w[:, sp_kr:sp_conv], zeros(LANE - KPE_OFF - QK_ROPE), w[:, sp_conv:]],
        axis=1)


def kernel(x, c, positions, ada_w, ada_b, norm1_g, norm2_g, w_in, q_norm_g, w_uq, kv_norm_g, w_ukv,
           w_o_attn, conv_w, conv_b, conv_ln_g, conv_ln_b, w_conv_out, w_out, router_group_w,
           router_group_b, router_expert_w, router_expert_b, expert_w_gate, expert_w_up,
           expert_w_down, final_norm_g):
    B, S, D = x.shape
    L = ada_w.shape[0]
    T = B * S

    c_pad = jnp.pad(c, ((0, 8 - B), (0, 0)))
    ada = _ada_call(c_pad, ada_w, ada_b)[:, :B].reshape(L, B, 6, D)

    inv_freq = ROPE_BASE ** (-jnp.arange(0, QK_ROPE, 2, dtype=F32) / QK_ROPE)
    freq_row = jnp.zeros((1, LANE), F32)
    freq_row = freq_row.at[0, KPE_OFF:KPE_OFF + QK_ROPE // 2].set(inv_freq)
    freq_row = freq_row.at[0, KPE_OFF + QK_ROPE // 2:KPE_OFF + QK_ROPE].set(inv_freq)
    tc, ts1, ts2 = _rope_call(positions.reshape(T, 1), freq_row)

    x2d = x.reshape(T, D)
    for l in range(L):
        w_in_p = _layout_w_in(w_in[l]).astype(BF16)
        w_uq_p = _pad_heads(w_uq[l], QK_DIM).astype(BF16)
        wkv = w_ukv[l].reshape(KV_LORA, N_HEADS, QK_NOPE + V_DIM)
        w_uk_p = _pad_heads(wkv[:, :, :QK_NOPE].reshape(KV_LORA, -1), QK_NOPE).astype(BF16)
        w_uvt = jnp.pad(wkv[:, :, QK_NOPE:], ((0, 0), (0, 0), (0, VT_ROWS - V_DIM)))
        w_uvt_p = w_uvt.reshape(KV_LORA, N_HEADS * VT_ROWS).T.astype(BF16)
        q, k, vt, u, gates = _inproj_call(
            x2d, ada[l], norm1_g[l].reshape(1, D), w_in_p, q_norm_g[l].reshape(1, -1), w_uq_p,
            kv_norm_g[l].reshape(1, -1), w_uk_p, w_uvt_p, tc, ts1, ts2, B, S)

        attn = _attn_call(q, k, vt).reshape(T, N_HEADS * V_DIM)

        w_r = jnp.concatenate([router_group_w[l], router_expert_w[l]], axis=1)
        w_r = jnp.pad(w_r, ((0, 0), (0, LANE - w_r.shape[1]))).astype(BF16)
        b_r = jnp.concatenate([router_group_b[l], router_expert_b[l]])
        b_r = jnp.pad(b_r, (0, LANE - b_r.shape[0])).reshape(1, LANE)
        cw = jnp.pad(conv_w[l], ((0, HALO - CONV_K), (0, 0)))
        x1, h2, comb = _mix_call(
            attn, u, gates, x2d, ada[l], w_o_attn[l].astype(BF16), cw, conv_b[l].reshape(1, -1),
            conv_ln_g[l].reshape(1, -1), conv_ln_b[l].reshape(1, -1), w_conv_out[l].astype(BF16),
            w_out[l].astype(BF16), norm2_g[l].reshape(1, D), w_r, b_r, S)

        EF = E_PER_G * D_EXPERT
        wg = expert_w_gate[l].transpose(0, 2, 1, 3).reshape(N_GROUPS, D, EF).astype(BF16)
        wu = expert_w_up[l].transpose(0, 2, 1, 3).reshape(N_GROUPS, D, EF).astype(BF16)
        wd = expert_w_down[l].reshape(N_GROUPS, EF, D).astype(BF16)
        x2d = _moe_call(h2, comb, x1, ada[l], wg, wu, wd, final_norm_g.reshape(1, D), S,
                        final_norm=(l == L - 1))
    return x2d.reshape(B, S, D)
```

```python
import functools
import math

import jax
import jax.numpy as jnp
from jax import lax
from jax.experimental import pallas as pl
from jax.experimental.pallas import tpu as pltpu

D_MODEL = 1024
N_HEADS = 8
QK_NOPE = 64
QK_ROPE = 32
V_DIM = 64
QK_DIM = QK_NOPE + QK_ROPE
Q_LORA = 384
KV_LORA = 256
ROPE_BASE = 10000.0
CONV_C = 512
CONV_K = 31
N_GROUPS = 4
E_PER_G = 8
D_EXPERT = 128
EPS = 1e-6

LANE = 128
SUBLANE = 8
CONV_TCHUNK = 128
HEAD_PAD = LANE
KPE_OFF = QK_NOPE
VT_ROWS = 80
HALO = 32
VMEM_LIMIT = 56 * 1024 * 1024

C_QLAT = 0
C_KVLAT = Q_LORA
C_KPE = Q_LORA + KV_LORA
C_GLU_A = C_KPE + LANE
C_GLU_B = C_GLU_A + CONV_C
C_GATE = C_GLU_B + CONV_C
IN_COLS_PAD = C_GATE + 2 * D_MODEL

TM_IN = 512
TM_MIX = 512
TM_MOE = 1024
TQ = 512
TK = 256

BF16 = jnp.bfloat16
F32 = jnp.float32


def _cparams(sem, flags=None):
    return pltpu.CompilerParams(dimension_semantics=sem, vmem_limit_bytes=VMEM_LIMIT, flags=flags)


def _ada_kernel(c_ref, w_ref, b_ref, o_ref):
    c = c_ref[...]
    c_act = c * jax.nn.sigmoid(c)
    o_ref[0] = jnp.dot(c_act, w_ref[0], preferred_element_type=F32,
                       precision=lax.Precision.HIGHEST) + b_ref[0]


def _ada_call(c_pad, ada_w, ada_b):
    L, D, N = ada_w.shape
    tn = 1536
    return pl.pallas_call(
        _ada_kernel,
        grid=(L, N // tn),
        in_specs=[
            pl.BlockSpec((8, D), lambda l, j: (0, 0)),
            pl.BlockSpec((1, D, tn), lambda l, j: (l, 0, j)),
            pl.BlockSpec((1, 1, tn), lambda l, j: (l, 0, j)),
        ],
        out_specs=pl.BlockSpec((1, 8, tn), lambda l, j: (l, 0, j)),
        out_shape=jax.ShapeDtypeStruct((L, 8, N), F32),
        compiler_params=_cparams(("arbitrary", "arbitrary")),
        name="ada_ln",
    )(c_pad, ada_w, ada_b.reshape(L, 1, N))


def _rope_kernel(pos_ref, freq_ref, c_ref, s1_ref, s2_ref):
    pos = pos_ref[...].astype(F32)
    ang = pos * freq_ref[...]
    lane = lax.broadcasted_iota(jnp.int32, ang.shape, 1)
    cosv = jnp.cos(ang)
    sinv = jnp.sin(ang)
    is_x1 = (lane >= KPE_OFF) & (lane < KPE_OFF + QK_ROPE // 2)
    is_x2 = (lane >= KPE_OFF + QK_ROPE // 2) & (lane < KPE_OFF + QK_ROPE)
    c_ref[...] = jnp.where(is_x1 | is_x2, cosv, jnp.where(lane < KPE_OFF, 1.0, 0.0))
    s1_ref[...] = jnp.where(is_x2, sinv, 0.0)
    s2_ref[...] = jnp.where(is_x1, -sinv, 0.0)


def _rope_call(pos_col, freq_row):
    T = pos_col.shape[0]
    tm = 2048
    spec = pl.BlockSpec((tm, LANE), lambda i: (i, 0))
    return pl.pallas_call(
        _rope_kernel,
        grid=(T // tm,),
        in_specs=[pl.BlockSpec((tm, 1), lambda i: (i, 0)),
                  pl.BlockSpec((1, LANE), lambda i: (0, 0))],
        out_specs=[spec, spec, spec],
        out_shape=[jax.ShapeDtypeStruct((T, LANE), F32)] * 3,
        compiler_params=_cparams(("arbitrary",)),
        name="rope_tables",
    )(pos_col, freq_row)


def _rms(x, g):
    ms = jnp.mean(x * x, axis=-1, keepdims=True)
    return x * lax.rsqrt(ms + EPS) * g


def _rope_apply(x, c, s1, s2):
    half = QK_ROPE // 2
    return x * c + pltpu.roll(x, half, axis=1) * s1 + pltpu.roll(x, LANE - half, axis=1) * s2


def _conv_ln_silu(buf_ref, cw_ref, cb_ref, lng_ref, lnb_ref, tm):
    base = HALO - (CONV_K - 1)
    wrows = CONV_TCHUNK + HALO
    pieces = []
    for t0 in range(0, tm, CONV_TCHUNK):
        row = []
        for c0 in range(0, CONV_C, LANE):
            cs = slice(c0, c0 + LANE)
            win = buf_ref[t0:t0 + wrows, cs]
            acc = jnp.broadcast_to(cb_ref[:, cs], (CONV_TCHUNK, LANE))
            for r in range(SUBLANE):
                shifted = win if r == 0 else pltpu.roll(win, wrows - r, axis=0)
                for o in range(base, base + CONV_K):
                    if o % SUBLANE == r:
                        acc = acc + shifted[o - r:o - r + CONV_TCHUNK] * cw_ref[o - base:o - base + 1, cs]
            row.append(acc)
        pieces.append(jnp.concatenate(row, axis=1))
    acc = jnp.concatenate(pieces, axis=0)
    mu = jnp.mean(acc, axis=-1, keepdims=True)
    xc = acc - mu
    var = jnp.mean(xc * xc, axis=-1, keepdims=True)
    ln = xc * lax.rsqrt(var + EPS) * lng_ref[...] + lnb_ref[...]
    return ln * jax.nn.sigmoid(ln)


def _inproj_kernel(x_ref, ada_ref, g1_ref, win_ref, qg_ref, wuq_ref, kvg_ref, wuk_ref, wuvt_ref,
                   c_ref, s1_ref, s2_ref, cw_ref, cb_ref, lng_ref, lnb_ref,
                   q_ref, k_ref, vt_ref, uc_ref, gate_ref, buf_ref, *, tiles_per_batch):
    tm = x_ref.shape[0]
    first = (pl.program_id(0) % tiles_per_batch) == 0

    @pl.when(first)
    def _():
        buf_ref[0:HALO, :] = jnp.zeros((HALO, CONV_C), F32)

    @pl.when(jnp.logical_not(first))
    def _():
        buf_ref[0:HALO, :] = buf_ref[tm:tm + HALO, :]

    x = x_ref[...]
    sh1 = ada_ref[0, 0:1, :]
    sc1 = ada_ref[0, 1:2, :]
    h = _rms(x, g1_ref[...]) * (1.0 + sc1) + sh1
    z = jnp.dot(h.astype(BF16), win_ref[...], preferred_element_type=F32)

    qn = _rms(z[:, C_QLAT:C_QLAT + Q_LORA], qg_ref[...]).astype(BF16)
    kvn = _rms(z[:, C_KVLAT:C_KVLAT + KV_LORA], kvg_ref[...]).astype(BF16)
    q = jnp.dot(qn, wuq_ref[...], preferred_element_type=F32)
    kk = jnp.dot(kvn, wuk_ref[...], preferred_element_type=F32)
    vvt = lax.dot_general(wuvt_ref[...], kvn, (((1,), (1,)), ((), ())), preferred_element_type=F32)
    vrow = lax.broadcasted_iota(jnp.int32, (VT_ROWS, 1), 0)
    ones_row = jnp.where(vrow == V_DIM, 1.0, 0.0)

    c = c_ref[...]
    s1 = s1_ref[...]
    s2 = s2_ref[...]
    qs = (QK_DIM ** -0.5) * math.log2(math.e)
    cq, s1q, s2q = c * qs, s1 * qs, s2 * qs
    kpe = _rope_apply(z[:, C_KPE:C_KPE + LANE], c, s1, s2)
    for hh in range(N_HEADS):
        sl = slice(hh * HEAD_PAD, (hh + 1) * HEAD_PAD)
        q_ref[0, hh] = _rope_apply(q[:, sl], cq, s1q, s2q).astype(BF16)
        k_ref[0, hh] = (kk[:, sl] + kpe).astype(BF16)
        vth = (vvt[hh * VT_ROWS:(hh + 1) * VT_ROWS, :] + ones_row).astype(BF16)
        for j in range(vt_ref.shape[2]):
            vt_ref[0, hh, j] = vth[:, j * TK:(j + 1) * TK]

    gate_ref[...] = jax.nn.sigmoid(z[:, C_GATE:C_GATE + 2 * D_MODEL]).astype(BF16)

    buf_ref[HALO:HALO + tm, :] = z[:, C_GLU_A:C_GLU_A + CONV_C] * jax.nn.sigmoid(z[:, C_GLU_B:C_GLU_B + CONV_C])
    uc_ref[...] = _conv_ln_silu(buf_ref, cw_ref, cb_ref, lng_ref, lnb_ref, tm).astype(BF16)


def _inproj_call(x2d, ada_l, g1, w_in_p, qg, w_uq_p, kvg, w_uk_p, w_uvt_p, tc, ts1, ts2,
                 cw, cb, lng, lnb, B, S):
    T, D = x2d.shape
    tm = TM_IN
    tpb = S // tm
    kpt = tm // TK
    const = lambda shape: pl.BlockSpec(shape, lambda i: (0,) * len(shape))
    head_spec = pl.BlockSpec((1, N_HEADS, tm, HEAD_PAD), lambda i: (i // tpb, 0, i % tpb, 0))
    head_shape = jax.ShapeDtypeStruct((B, N_HEADS, S, HEAD_PAD), BF16)
    tab = pl.BlockSpec((tm, LANE), lambda i: (i, 0))
    return pl.pallas_call(
        functools.partial(_inproj_kernel, tiles_per_batch=tpb),
        grid=(T // tm,),
        in_specs=[
            pl.BlockSpec((tm, D), lambda i: (i, 0)),
            pl.BlockSpec((1, 6, D), lambda i: (i // tpb, 0, 0)),
            const((1, D)),
            const((D, IN_COLS_PAD)),
            const((1, Q_LORA)),
            const((Q_LORA, N_HEADS * HEAD_PAD)),
            const((1, KV_LORA)),
            const((KV_LORA, N_HEADS * HEAD_PAD)),
            const((N_HEADS * VT_ROWS, KV_LORA)),
            tab, tab, tab,
            const((HALO, CONV_C)), const((1, CONV_C)), const((1, CONV_C)), const((1, CONV_C)),
        ],
        out_specs=[head_spec, head_spec,
                   pl.BlockSpec((1, N_HEADS, kpt, VT_ROWS, TK), lambda i: (i // tpb, 0, i % tpb, 0, 0)),
                   pl.BlockSpec((tm, CONV_C), lambda i: (i, 0)),
                   pl.BlockSpec((tm, 2 * D_MODEL), lambda i: (i, 0))],
        out_shape=[head_shape, head_shape,
                   jax.ShapeDtypeStruct((B, N_HEADS, S // TK, VT_ROWS, TK), BF16),
                   jax.ShapeDtypeStruct((T, CONV_C), BF16),
                   jax.ShapeDtypeStruct((T, 2 * D_MODEL), BF16)],
        scratch_shapes=[pltpu.VMEM((HALO + tm, CONV_C), F32)],
        compiler_params=_cparams(("arbitrary",)),
        name="in_proj",
    )(x2d, ada_l, g1, w_in_p, qg, w_uq_p, kvg, w_uk_p, w_uvt_p, tc, ts1, ts2, cw, cb, lng, lnb)


HEADS_PER_STEP = 8


def _attn_kernel(q_ref, k_ref, vt_ref, o_ref, *scratch):
    qi = pl.program_id(2)
    heads = range(HEADS_PER_STEP)
    assert TQ == 2 * TK
    hs = HEADS_PER_STEP
    s_ref = [scratch[2 * hh:2 * hh + 2] for hh in heads]
    p_ref = [scratch[2 * hs + 2 * hh:2 * hs + 2 * hh + 2] for hh in heads]
    acc_ref = scratch[4 * hs:5 * hs]

    def qk(kb, slot, qlo=0):
        for hh in heads:
            k = k_ref[0, hh, pl.ds(kb * TK, TK), :]
            s_ref[hh][slot][:, qlo:] = lax.dot_general(k, q_ref[0, hh, qlo:, :], (((1,), (1,)), ((), ())),
                                                       preferred_element_type=F32)

    def pv(kb, slot, alpha, qlo=0):
        for hh in heads:
            acc_ref[hh][:, qlo:] = (acc_ref[hh][:, qlo:] * alpha[hh]
                                    + jnp.dot(vt_ref[0, hh, kb], p_ref[hh][slot][:, qlo:],
                                              preferred_element_type=F32))

    def softmax(slot, m, mask=None, qlo=0):
        m_out, alpha = [], []
        for hh in heads:
            s = s_ref[hh][slot][:, qlo:]
            if mask is not None:
                s = jnp.where(mask[:, qlo:], s, -jnp.inf)
            m_old = m[hh][:, qlo:]
            m_new = jnp.maximum(m_old, jnp.max(s, axis=0, keepdims=True))
            p_ref[hh][slot][:, qlo:] = jnp.exp2(s - m_new).astype(BF16)
            alpha.append(jnp.exp2(m_old - m_new))
            m_out.append(m_new)
        return tuple(m_out), tuple(alpha)

    for hh in heads:
        acc_ref[hh][...] = jnp.zeros((VT_ROWS, TQ), F32)
        p_ref[hh][1][...] = jnp.zeros((TK, TQ), BF16)
    m0 = tuple(jnp.full((1, TQ), -jnp.inf, F32) for _ in heads)
    one = tuple(jnp.ones((1, TQ), F32) for _ in heads)
    qk(0, 0)

    def body(i, carry):
        m, alpha1 = carry
        a = 2 * i
        qk(a + 1, 1)
        pv(jnp.maximum(a - 1, 0), 1, alpha1)
        m, alpha0 = softmax(0, m)
        qk(a + 2, 0)
        pv(a, 0, alpha0)
        m, alpha1 = softmax(1, m)
        return m, alpha1

    m, alpha1 = lax.fori_loop(0, qi, body, (m0, one))
    a = 2 * qi
    key = lax.broadcasted_iota(jnp.int32, (TK, TQ), 0)
    qry = lax.broadcasted_iota(jnp.int32, (TK, TQ), 1)
    qk(a + 1, 1, qlo=TK)
    pv(jnp.maximum(a - 1, 0), 1, alpha1)
    m, alpha0 = softmax(0, m, key <= qry)
    pv(a, 0, alpha0)
    _, alpha1 = softmax(1, m, key + TK <= qry, qlo=TK)
    pv(a + 1, 1, alpha1, qlo=TK)
    outs = []
    for hh in heads:
        acc = acc_ref[hh][...]
        o_t = acc[:V_DIM, :] / acc[V_DIM:V_DIM + 1, :]
        outs.append(o_t.T)
    o_ref[0] = jnp.concatenate(outs, axis=-1).astype(BF16)


def _attn_call(q, k, vt):
    B, H, S, _ = q.shape
    nq = S // TQ
    hs = HEADS_PER_STEP
    return pl.pallas_call(
        _attn_kernel,
        grid=(B, H // hs, nq),
        in_specs=[
            pl.BlockSpec((1, hs, TQ, HEAD_PAD), lambda b, h, i: (b, h, i, 0)),
            pl.BlockSpec((1, hs, S, HEAD_PAD), lambda b, h, i: (b, h, 0, 0),
                         pipeline_mode=pl.Buffered(1)),
            pl.BlockSpec((1, hs, S // TK, VT_ROWS, TK), lambda b, h, i: (b, h, 0, 0, 0),
                         pipeline_mode=pl.Buffered(1)),
        ],
        out_specs=pl.BlockSpec((1, TQ, hs * V_DIM), lambda b, h, i: (b, i, h)),
        out_shape=jax.ShapeDtypeStruct((B, S, H * V_DIM), BF16),
        scratch_shapes=([pltpu.VMEM((TK, TQ), F32)] * (2 * hs) + [pltpu.VMEM((TK, TQ), BF16)] * (2 * hs)
                        + [pltpu.VMEM((VT_ROWS, TQ), F32)] * hs),
        compiler_params=_cparams(("arbitrary", "arbitrary", "arbitrary")),
        name="mla_attention",
    )(q, k, vt)


def _mix_kernel(attn_ref, uc_ref, gate_ref, x_ref, ada_ref, woa_ref, wco_ref, wout_ref, g2_ref,
                wr_ref, br_ref, x1_ref, h2_ref, comb_ref):
    y_conv = jnp.dot(uc_ref[...], wco_ref[...], preferred_element_type=F32)
    y_attn = jnp.dot(attn_ref[...], woa_ref[...], preferred_element_type=F32)

    g_attn = gate_ref[:, 0:D_MODEL].astype(F32)
    g_conv = gate_ref[:, D_MODEL:2 * D_MODEL].astype(F32)
    merged = g_attn * y_attn + g_conv * y_conv
    y = jnp.dot(merged.astype(BF16), wout_ref[...], preferred_element_type=F32)
    gt1 = ada_ref[0, 2:3, :]
    x1 = x_ref[...] + gt1 * y
    x1_ref[...] = x1

    sh2 = ada_ref[0, 3:4, :]
    sc2 = ada_ref[0, 4:5, :]
    h2 = (_rms(x1, g2_ref[...]) * (1.0 + sc2) + sh2).astype(BF16)
    h2_ref[...] = h2

    logits = jnp.dot(h2, wr_ref[...], preferred_element_type=F32) + br_ref[...]
    lane = lax.broadcasted_iota(jnp.int32, logits.shape, 1)
    big = jnp.int32(1 << 20)
    neg = -jnp.inf
    lg = jnp.where(lane < N_GROUPS, logits, neg)
    gmax = jnp.max(lg, axis=-1, keepdims=True)
    g_val = 1.0 / jnp.sum(jnp.exp(lg - gmax), axis=-1, keepdims=True)
    g_idx = jnp.min(jnp.where(lg == gmax, lane, big), axis=-1, keepdims=True)
    e_lo = N_GROUPS + g_idx * E_PER_G
    emask = (lane >= e_lo) & (lane < e_lo + E_PER_G)
    le = jnp.where(emask, logits, neg)
    m1 = jnp.max(le, axis=-1, keepdims=True)
    i1 = jnp.min(jnp.where(le == m1, lane, big), axis=-1, keepdims=True)
    le2 = jnp.where(lane == i1, neg, le)
    m2 = jnp.max(le2, axis=-1, keepdims=True)
    i2 = jnp.min(jnp.where(le2 == m2, lane, big), axis=-1, keepdims=True)
    z_e = jnp.sum(jnp.exp(le - m1), axis=-1, keepdims=True)
    p1 = 1.0 / z_e
    p2 = jnp.exp(m2 - m1) / z_e
    w1 = g_val * (p1 / (p1 + p2))
    w2 = g_val * (p2 / (p1 + p2))
    comb = jnp.where(lane == i1, w1, 0.0) + jnp.where(lane == i2, w2, 0.0)
    for g in range(N_GROUPS):
        shifted = pltpu.roll(comb, LANE - (N_GROUPS + g * E_PER_G), axis=1)
        comb_ref[:, g * LANE:(g + 1) * LANE] = jnp.where(lane < E_PER_G, shifted, 0.0)


def _mix_call(attn2d, uc2d, gates, x2d, ada_l, woa, wco, wout, g2, wr, br, S):
    T, D = x2d.shape
    tm = TM_MIX
    tpb = S // tm
    const = lambda shape: pl.BlockSpec(shape, lambda i: (0,) * len(shape))
    row = lambda w: pl.BlockSpec((tm, w), lambda i: (i, 0))
    return pl.pallas_call(
        _mix_kernel,
        grid=(T // tm,),
        in_specs=[
            row(N_HEADS * V_DIM),
            row(CONV_C),
            row(2 * D),
            row(D),
            pl.BlockSpec((1, 6, D), lambda i: (i // tpb, 0, 0)),
            const((N_HEADS * V_DIM, D)),
            const((CONV_C, D)),
            const((D, D)),
            const((1, D)),
            const((D, LANE)),
            const((1, LANE)),
        ],
        out_specs=[row(D), row(D), row(N_GROUPS * LANE)],
        out_shape=[jax.ShapeDtypeStruct((T, D), F32),
                   jax.ShapeDtypeStruct((T, D), BF16),
                   jax.ShapeDtypeStruct((T, N_GROUPS * LANE), F32)],
        compiler_params=_cparams(("arbitrary",)),
        name="mixer_merge",
    )(attn2d, uc2d, gates, x2d, ada_l, woa, wco, wout, g2, wr, br)


def _moe_kernel(h_ref, comb_ref, x1_ref, ada_ref, wg_ref, wu_ref, wd_ref, gf_ref, o_ref, acc_ref,
                *, final_norm):
    g = pl.program_id(1)

    @pl.when(g == 0)
    def _():
        acc_ref[...] = jnp.zeros_like(acc_ref)

    h = h_ref[...]
    a = jnp.dot(h, wg_ref[0], preferred_element_type=F32)
    u = jnp.dot(h, wu_ref[0], preferred_element_type=F32)
    cw = comb_ref[...]
    tm = h.shape[0]
    parts = []
    for e in range(E_PER_G):
        sl = slice(e * D_EXPERT, (e + 1) * D_EXPERT)
        ae = a[:, sl]
        ce = jnp.broadcast_to(cw[:, e:e + 1], (tm, D_EXPERT))
        parts.append((ae * jax.nn.sigmoid(ae) * u[:, sl] * ce).astype(BF16))
    hid = jnp.concatenate(parts, axis=-1)
    acc_ref[...] += jnp.dot(hid, wd_ref[0], preferred_element_type=F32)

    @pl.when(g == N_GROUPS - 1)
    def _():
        gt2 = ada_ref[0, 5:6, :]
        x2 = x1_ref[...] + gt2 * acc_ref[...]
        if final_norm:
            x2 = _rms(x2, gf_ref[...])
        o_ref[...] = x2


def _moe_call(h2, comb, x1, ada_l, wg, wu, wd, gf, S, final_norm):
    T, D = x1.shape
    tm = TM_MOE
    tpb = S // tm
    EF = E_PER_G * D_EXPERT
    return pl.pallas_call(
        functools.partial(_moe_kernel, final_norm=final_norm),
        grid=(T // tm, N_GROUPS),
        in_specs=[
            pl.BlockSpec((tm, D), lambda i, g: (i, 0)),
            pl.BlockSpec((tm, LANE), lambda i, g: (i, g)),
            pl.BlockSpec((tm, D), lambda i, g: (i, 0)),
            pl.BlockSpec((1, 6, D), lambda i, g: (i // tpb, 0, 0)),
            pl.BlockSpec((1, D, EF), lambda i, g: (g, 0, 0)),
            pl.BlockSpec((1, D, EF), lambda i, g: (g, 0, 0)),
            pl.BlockSpec((1, EF, D), lambda i, g: (g, 0, 0)),
            pl.BlockSpec((1, D), lambda i, g: (0, 0)),
        ],
        out_specs=pl.BlockSpec((tm, D), lambda i, g: (i, 0)),
        out_shape=jax.ShapeDtypeStruct((T, D), F32),
        scratch_shapes=[pltpu.VMEM((tm, D), F32)],
        compiler_params=_cparams(("arbitrary", "arbitrary")),
        name="moe_experts",
    )(h2, comb, x1, ada_l, wg, wu, wd, gf)


def _pad_heads(w, width):
    k = w.shape[0]
    w = w.reshape(k, N_HEADS, width)
    w = jnp.pad(w, ((0, 0), (0, 0), (0, HEAD_PAD - width)))
    return w.reshape(k, N_HEADS * HEAD_PAD)


def _layout_w_in(w):
    d = w.shape[0]
    sp_kr = Q_LORA + KV_LORA
    sp_conv = sp_kr + QK_ROPE
    zeros = lambda n: jnp.zeros((d, n), w.dtype)
    return jnp.concatenate(
        [w[:, :sp_kr], zeros(KPE_OFF), w[:, sp_kr:sp_conv], zeros(LANE - KPE_OFF - QK_ROPE), w[:, sp_conv:]],
        axis=1)


def kernel(x, c, positions, ada_w, ada_b, norm1_g, norm2_g, w_in, q_norm_g, w_uq, kv_norm_g, w_ukv,
           w_o_attn, conv_w, conv_b, conv_ln_g, conv_ln_b, w_conv_out, w_out, router_group_w,
           router_group_b, router_expert_w, router_expert_b, expert_w_gate, expert_w_up,
           expert_w_down, final_norm_g):
    B, S, D = x.shape
    L = ada_w.shape[0]
    T = B * S

    c_pad = jnp.pad(c, ((0, 8 - B), (0, 0)))
    ada = _ada_call(c_pad, ada_w, ada_b)[:, :B].reshape(L, B, 6, D)

    inv_freq = ROPE_BASE ** (-jnp.arange(0, QK_ROPE, 2, dtype=F32) / QK_ROPE)
    freq_row = jnp.zeros((1, LANE), F32)
    freq_row = freq_row.at[0, KPE_OFF:KPE_OFF + QK_ROPE // 2].set(inv_freq)
    freq_row = freq_row.at[0, KPE_OFF + QK_ROPE // 2:KPE_OFF + QK_ROPE].set(inv_freq)
    tc, ts1, ts2 = _rope_call(positions.reshape(T, 1), freq_row)

    x2d = x.reshape(T, D)
    for l in range(L):
        w_in_p = _layout_w_in(w_in[l]).astype(BF16)
        w_uq_p = _pad_heads(w_uq[l], QK_DIM).astype(BF16)
        wkv = w_ukv[l].reshape(KV_LORA, N_HEADS, QK_NOPE + V_DIM)
        w_uk_p = _pad_heads(wkv[:, :, :QK_NOPE].reshape(KV_LORA, -1), QK_NOPE).astype(BF16)
        w_uvt = jnp.pad(wkv[:, :, QK_NOPE:], ((0, 0), (0, 0), (0, VT_ROWS - V_DIM)))
        w_uvt_p = w_uvt.reshape(KV_LORA, N_HEADS * VT_ROWS).T.astype(BF16)
        cw = jnp.pad(conv_w[l], ((0, HALO - CONV_K), (0, 0)))
        q, k, vt, uc, gates = _inproj_call(
            x2d, ada[l], norm1_g[l].reshape(1, D), w_in_p, q_norm_g[l].reshape(1, -1), w_uq_p,
            kv_norm_g[l].reshape(1, -1), w_uk_p, w_uvt_p, tc, ts1, ts2,
            cw, conv_b[l].reshape(1, -1), conv_ln_g[l].reshape(1, -1), conv_ln_b[l].reshape(1, -1), B, S)

        attn = _attn_call(q, k, vt).reshape(T, N_HEADS * V_DIM)

        w_r = jnp.concatenate([router_group_w[l], router_expert_w[l]], axis=1)
        w_r = jnp.pad(w_r, ((0, 0), (0, LANE - w_r.shape[1]))).astype(BF16)
        b_r = jnp.concatenate([router_group_b[l], router_expert_b[l]])
        b_r = jnp.pad(b_r, (0, LANE - b_r.shape[0])).reshape(1, LANE)
        x1, h2, comb = _mix_call(
            attn, uc, gates, x2d, ada[l], w_o_attn[l].astype(BF16), w_conv_out[l].astype(BF16),
            w_out[l].astype(BF16), norm2_g[l].reshape(1, D), w_r, b_r, S)

        EF = E_PER_G * D_EXPERT
        wg = expert_w_gate[l].transpose(0, 2, 1, 3).reshape(N_GROUPS, D, EF).astype(BF16)
        wu = expert_w_up[l].transpose(0, 2, 1, 3).reshape(N_GROUPS, D, EF).astype(BF16)
        wd = expert_w_down[l].reshape(N_GROUPS, EF, D).astype(BF16)
        x2d = _moe_call(h2, comb, x1, ada[l], wg, wu, wd, final_norm_g.reshape(1, D), S,
                        final_norm=(l == L - 1))
    return x2d.reshape(B, S, D)
```

```python
import functools
import math

import jax
import jax.numpy as jnp
from jax import lax
from jax.experimental import pallas as pl
from jax.experimental.pallas import tpu as pltpu

D_MODEL = 1024
N_HEADS = 8
QK_NOPE = 64
QK_ROPE = 32
V_DIM = 64
QK_DIM = QK_NOPE + QK_ROPE
Q_LORA = 384
KV_LORA = 256
ROPE_BASE = 10000.0
CONV_C = 512
CONV_K = 31
N_GROUPS = 4
E_PER_G = 8
D_EXPERT = 128
EPS = 1e-6

LANE = 128
SUBLANE = 8
CONV_TCHUNK = 128
HEAD_PAD = LANE
KPE_OFF = QK_NOPE
VT_ROWS = 80
HALO = 32
VMEM_LIMIT = 56 * 1024 * 1024

C_QLAT = 0
C_KVLAT = Q_LORA
C_KPE = Q_LORA + KV_LORA
C_GLU_A = C_KPE + LANE
C_GLU_B = C_GLU_A + CONV_C
C_GATE = C_GLU_B + CONV_C
IN_COLS_PAD = C_GATE + 2 * D_MODEL

TM_IN = 512
TM_MIX = 512
MOE_CHUNK = 128
TQ = 512
TK = 256

BF16 = jnp.bfloat16
F32 = jnp.float32


def _cparams(sem, flags=None):
    return pltpu.CompilerParams(dimension_semantics=sem, vmem_limit_bytes=VMEM_LIMIT, flags=flags)


def _ada_kernel(c_ref, w_ref, b_ref, o_ref):
    c = c_ref[...]
    c_act = c * jax.nn.sigmoid(c)
    o_ref[0] = jnp.dot(c_act, w_ref[0], preferred_element_type=F32,
                       precision=lax.Precision.HIGHEST) + b_ref[0]


def _ada_call(c_pad, ada_w, ada_b):
    L, D, N = ada_w.shape
    tn = 1536
    return pl.pallas_call(
        _ada_kernel,
        grid=(L, N // tn),
        in_specs=[
            pl.BlockSpec((8, D), lambda l, j: (0, 0)),
            pl.BlockSpec((1, D, tn), lambda l, j: (l, 0, j)),
            pl.BlockSpec((1, 1, tn), lambda l, j: (l, 0, j)),
        ],
        out_specs=pl.BlockSpec((1, 8, tn), lambda l, j: (l, 0, j)),
        out_shape=jax.ShapeDtypeStruct((L, 8, N), F32),
        compiler_params=_cparams(("arbitrary", "arbitrary")),
        name="ada_ln",
    )(c_pad, ada_w, ada_b.reshape(L, 1, N))


def _rope_kernel(pos_ref, freq_ref, c_ref, s1_ref, s2_ref):
    pos = pos_ref[...].astype(F32)
    ang = pos * freq_ref[...]
    lane = lax.broadcasted_iota(jnp.int32, ang.shape, 1)
    cosv = jnp.cos(ang)
    sinv = jnp.sin(ang)
    is_x1 = (lane >= KPE_OFF) & (lane < KPE_OFF + QK_ROPE // 2)
    is_x2 = (lane >= KPE_OFF + QK_ROPE // 2) & (lane < KPE_OFF + QK_ROPE)
    c_ref[...] = jnp.where(is_x1 | is_x2, cosv, jnp.where(lane < KPE_OFF, 1.0, 0.0))
    s1_ref[...] = jnp.where(is_x2, sinv, 0.0)
    s2_ref[...] = jnp.where(is_x1, -sinv, 0.0)


def _rope_call(pos_col, freq_row):
    T = pos_col.shape[0]
    tm = 2048
    spec = pl.BlockSpec((tm, LANE), lambda i: (i, 0))
    return pl.pallas_call(
        _rope_kernel,
        grid=(T // tm,),
        in_specs=[pl.BlockSpec((tm, 1), lambda i: (i, 0)),
                  pl.BlockSpec((1, LANE), lambda i: (0, 0))],
        out_specs=[spec, spec, spec],
        out_shape=[jax.ShapeDtypeStruct((T, LANE), F32)] * 3,
        compiler_params=_cparams(("arbitrary",)),
        name="rope_tables",
    )(pos_col, freq_row)


def _rms(x, g):
    ms = jnp.mean(x * x, axis=-1, keepdims=True)
    return x * lax.rsqrt(ms + EPS) * g


def _rope_apply(x, c, s1, s2):
    half = QK_ROPE // 2
    return x * c + pltpu.roll(x, half, axis=1) * s1 + pltpu.roll(x, LANE - half, axis=1) * s2


def _conv_ln_silu(buf_ref, cw_ref, cb_ref, lng_ref, lnb_ref, tm):
    base = HALO - (CONV_K - 1)
    wrows = CONV_TCHUNK + HALO
    pieces = []
    for t0 in range(0, tm, CONV_TCHUNK):
        row = []
        for c0 in range(0, CONV_C, LANE):
            cs = slice(c0, c0 + LANE)
            win = buf_ref[t0:t0 + wrows, cs]
            acc = jnp.broadcast_to(cb_ref[:, cs], (CONV_TCHUNK, LANE))
            for r in range(SUBLANE):
                shifted = win if r == 0 else pltpu.roll(win, wrows - r, axis=0)
                for o in range(base, base + CONV_K):
                    if o % SUBLANE == r:
                        acc = acc + shifted[o - r:o - r + CONV_TCHUNK] * cw_ref[o - base:o - base + 1, cs]
            row.append(acc)
        pieces.append(jnp.concatenate(row, axis=1))
    acc = jnp.concatenate(pieces, axis=0)
    mu = jnp.mean(acc, axis=-1, keepdims=True)
    xc = acc - mu
    var = jnp.mean(xc * xc, axis=-1, keepdims=True)
    ln = xc * lax.rsqrt(var + EPS) * lng_ref[...] + lnb_ref[...]
    return ln * jax.nn.sigmoid(ln)


def _inproj_kernel(x_ref, ada_ref, g1_ref, win_ref, qg_ref, wuq_ref, kvg_ref, wuk_ref, wuvt_ref,
                   c_ref, s1_ref, s2_ref, cw_ref, cb_ref, lng_ref, lnb_ref,
                   q_ref, k_ref, vt_ref, uc_ref, gate_ref, buf_ref, *, tiles_per_batch):
    tm = x_ref.shape[0]
    first = (pl.program_id(0) % tiles_per_batch) == 0

    @pl.when(first)
    def _():
        buf_ref[0:HALO, :] = jnp.zeros((HALO, CONV_C), F32)

    @pl.when(jnp.logical_not(first))
    def _():
        buf_ref[0:HALO, :] = buf_ref[tm:tm + HALO, :]

    x = x_ref[...]
    sh1 = ada_ref[0, 0:1, :]
    sc1 = ada_ref[0, 1:2, :]
    h = _rms(x, g1_ref[...]) * (1.0 + sc1) + sh1
    z = jnp.dot(h.astype(BF16), win_ref[...], preferred_element_type=F32)

    qn = _rms(z[:, C_QLAT:C_QLAT + Q_LORA], qg_ref[...]).astype(BF16)
    kvn = _rms(z[:, C_KVLAT:C_KVLAT + KV_LORA], kvg_ref[...]).astype(BF16)
    q = jnp.dot(qn, wuq_ref[...], preferred_element_type=F32)
    kk = jnp.dot(kvn, wuk_ref[...], preferred_element_type=F32)
    vvt = lax.dot_general(wuvt_ref[...], kvn, (((1,), (1,)), ((), ())), preferred_element_type=F32)
    vrow = lax.broadcasted_iota(jnp.int32, (VT_ROWS, 1), 0)
    ones_row = jnp.where(vrow == V_DIM, 1.0, 0.0)

    c = c_ref[...]
    s1 = s1_ref[...]
    s2 = s2_ref[...]
    qs = (QK_DIM ** -0.5) * math.log2(math.e)
    cq, s1q, s2q = c * qs, s1 * qs, s2 * qs
    kpe = _rope_apply(z[:, C_KPE:C_KPE + LANE], c, s1, s2)
    for hh in range(N_HEADS):
        sl = slice(hh * HEAD_PAD, (hh + 1) * HEAD_PAD)
        q_ref[0, hh] = _rope_apply(q[:, sl], cq, s1q, s2q).astype(BF16)
        k_ref[0, hh] = (kk[:, sl] + kpe).astype(BF16)
        vth = (vvt[hh * VT_ROWS:(hh + 1) * VT_ROWS, :] + ones_row).astype(BF16)
        for j in range(vt_ref.shape[2]):
            vt_ref[0, hh, j] = vth[:, j * TK:(j + 1) * TK]

    gate_ref[...] = jax.nn.sigmoid(z[:, C_GATE:C_GATE + 2 * D_MODEL]).astype(BF16)

    buf_ref[HALO:HALO + tm, :] = z[:, C_GLU_A:C_GLU_A + CONV_C] * jax.nn.sigmoid(z[:, C_GLU_B:C_GLU_B + CONV_C])
    uc_ref[...] = _conv_ln_silu(buf_ref, cw_ref, cb_ref, lng_ref, lnb_ref, tm).astype(BF16)


def _inproj_call(x2d, ada_l, g1, w_in_p, qg, w_uq_p, kvg, w_uk_p, w_uvt_p, tc, ts1, ts2,
                 cw, cb, lng, lnb, B, S):
    T, D = x2d.shape
    tm = TM_IN
    tpb = S // tm
    kpt = tm // TK
    const = lambda shape: pl.BlockSpec(shape, lambda i: (0,) * len(shape))
    head_spec = pl.BlockSpec((1, N_HEADS, tm, HEAD_PAD), lambda i: (i // tpb, 0, i % tpb, 0))
    head_shape = jax.ShapeDtypeStruct((B, N_HEADS, S, HEAD_PAD), BF16)
    tab = pl.BlockSpec((tm, LANE), lambda i: (i, 0))
    return pl.pallas_call(
        functools.partial(_inproj_kernel, tiles_per_batch=tpb),
        grid=(T // tm,),
        in_specs=[
            pl.BlockSpec((tm, D), lambda i: (i, 0)),
            pl.BlockSpec((1, 6, D), lambda i: (i // tpb, 0, 0)),
            const((1, D)),
            const((D, IN_COLS_PAD)),
            const((1, Q_LORA)),
            const((Q_LORA, N_HEADS * HEAD_PAD)),
            const((1, KV_LORA)),
            const((KV_LORA, N_HEADS * HEAD_PAD)),
            const((N_HEADS * VT_ROWS, KV_LORA)),
            tab, tab, tab,
            const((HALO, CONV_C)), const((1, CONV_C)), const((1, CONV_C)), const((1, CONV_C)),
        ],
        out_specs=[head_spec, head_spec,
                   pl.BlockSpec((1, N_HEADS, kpt, VT_ROWS, TK), lambda i: (i // tpb, 0, i % tpb, 0, 0)),
                   pl.BlockSpec((tm, CONV_C), lambda i: (i, 0)),
                   pl.BlockSpec((tm, 2 * D_MODEL), lambda i: (i, 0))],
        out_shape=[head_shape, head_shape,
                   jax.ShapeDtypeStruct((B, N_HEADS, S // TK, VT_ROWS, TK), BF16),
                   jax.ShapeDtypeStruct((T, CONV_C), BF16),
                   jax.ShapeDtypeStruct((T, 2 * D_MODEL), BF16)],
        scratch_shapes=[pltpu.VMEM((HALO + tm, CONV_C), F32)],
        compiler_params=_cparams(("arbitrary",)),
        name="in_proj",
    )(x2d, ada_l, g1, w_in_p, qg, w_uq_p, kvg, w_uk_p, w_uvt_p, tc, ts1, ts2, cw, cb, lng, lnb)


HEADS_PER_STEP = 8


def _attn_kernel(q_ref, k_ref, vt_ref, o_ref, *scratch):
    qi = pl.program_id(2)
    heads = range(HEADS_PER_STEP)
    assert TQ == 2 * TK
    hs = HEADS_PER_STEP
    s_ref = [scratch[2 * hh:2 * hh + 2] for hh in heads]
    p_ref = [scratch[2 * hs + 2 * hh:2 * hs + 2 * hh + 2] for hh in heads]
    acc_ref = scratch[4 * hs:5 * hs]

    def qk(kb, slot, qlo=0):
        for hh in heads:
            k = k_ref[0, hh, pl.ds(kb * TK, TK), :]
            s_ref[hh][slot][:, qlo:] = lax.dot_general(k, q_ref[0, hh, qlo:, :], (((1,), (1,)), ((), ())),
                                                       preferred_element_type=F32)

    def pv(kb, slot, alpha, qlo=0):
        for hh in heads:
            acc_ref[hh][:, qlo:] = (acc_ref[hh][:, qlo:] * alpha[hh]
                                    + jnp.dot(vt_ref[0, hh, kb], p_ref[hh][slot][:, qlo:],
                                              preferred_element_type=F32))

    def softmax(slot, m, mask=None, qlo=0):
        m_out, alpha = [], []
        for hh in heads:
            s = s_ref[hh][slot][:, qlo:]
            if mask is not None:
                s = jnp.where(mask[:, qlo:], s, -jnp.inf)
            m_old = m[hh][:, qlo:]
            m_new = jnp.maximum(m_old, jnp.max(s, axis=0, keepdims=True))
            p_ref[hh][slot][:, qlo:] = jnp.exp2(s - m_new).astype(BF16)
            alpha.append(jnp.exp2(m_old - m_new))
            m_out.append(m_new)
        return tuple(m_out), tuple(alpha)

    for hh in heads:
        acc_ref[hh][...] = jnp.zeros((VT_ROWS, TQ), F32)
        p_ref[hh][1][...] = jnp.zeros((TK, TQ), BF16)
    m0 = tuple(jnp.full((1, TQ), -jnp.inf, F32) for _ in heads)
    one = tuple(jnp.ones((1, TQ), F32) for _ in heads)
    qk(0, 0)

    def body(i, carry):
        m, alpha1 = carry
        a = 2 * i
        qk(a + 1, 1)
        pv(jnp.maximum(a - 1, 0), 1, alpha1)
        m, alpha0 = softmax(0, m)
        qk(a + 2, 0)
        pv(a, 0, alpha0)
        m, alpha1 = softmax(1, m)
        return m, alpha1

    m, alpha1 = lax.fori_loop(0, qi, body, (m0, one))
    a = 2 * qi
    key = lax.broadcasted_iota(jnp.int32, (TK, TQ), 0)
    qry = lax.broadcasted_iota(jnp.int32, (TK, TQ), 1)
    qk(a + 1, 1, qlo=TK)
    pv(jnp.maximum(a - 1, 0), 1, alpha1)
    m, alpha0 = softmax(0, m, key <= qry)
    pv(a, 0, alpha0)
    _, alpha1 = softmax(1, m, key + TK <= qry, qlo=TK)
    pv(a + 1, 1, alpha1, qlo=TK)
    outs = []
    for hh in heads:
        acc = acc_ref[hh][...]
        o_t = acc[:V_DIM, :] / acc[V_DIM:V_DIM + 1, :]
        outs.append(o_t.T)
    o_ref[0] = jnp.concatenate(outs, axis=-1).astype(BF16)


def _attn_call(q, k, vt):
    B, H, S, _ = q.shape
    nq = S // TQ
    hs = HEADS_PER_STEP
    return pl.pallas_call(
        _attn_kernel,
        grid=(B, H // hs, nq),
        in_specs=[
            pl.BlockSpec((1, hs, TQ, HEAD_PAD), lambda b, h, i: (b, h, i, 0)),
            pl.BlockSpec((1, hs, S, HEAD_PAD), lambda b, h, i: (b, h, 0, 0),
                         pipeline_mode=pl.Buffered(1)),
            pl.BlockSpec((1, hs, S // TK, VT_ROWS, TK), lambda b, h, i: (b, h, 0, 0, 0),
                         pipeline_mode=pl.Buffered(1)),
        ],
        out_specs=pl.BlockSpec((1, TQ, hs * V_DIM), lambda b, h, i: (b, i, h)),
        out_shape=jax.ShapeDtypeStruct((B, S, H * V_DIM), BF16),
        scratch_shapes=([pltpu.VMEM((TK, TQ), F32)] * (2 * hs) + [pltpu.VMEM((TK, TQ), BF16)] * (2 * hs)
                        + [pltpu.VMEM((VT_ROWS, TQ), F32)] * hs),
        compiler_params=_cparams(("arbitrary", "arbitrary", "arbitrary")),
        name="mla_attention",
    )(q, k, vt)


def _mix_kernel(attn_ref, uc_ref, gate_ref, x_ref, ada_ref, woa_ref, wco_ref, wout_ref, g2_ref,
                wr_ref, br_ref, triu_ref, x1_ref, h2_ref, comb_ref, rrow_ref, cnt_ref):
    y_conv = jnp.dot(uc_ref[...], wco_ref[...], preferred_element_type=F32)
    y_attn = jnp.dot(attn_ref[...], woa_ref[...], preferred_element_type=F32)

    g_attn = gate_ref[:, 0:D_MODEL].astype(F32)
    g_conv = gate_ref[:, D_MODEL:2 * D_MODEL].astype(F32)
    merged = g_attn * y_attn + g_conv * y_conv
    y = jnp.dot(merged.astype(BF16), wout_ref[...], preferred_element_type=F32)
    gt1 = ada_ref[0, 2:3, :]
    x1 = x_ref[...] + gt1 * y
    x1_ref[...] = x1

    sh2 = ada_ref[0, 3:4, :]
    sc2 = ada_ref[0, 4:5, :]
    h2 = (_rms(x1, g2_ref[...]) * (1.0 + sc2) + sh2).astype(BF16)
    h2_ref[...] = h2

    logits = jnp.dot(h2, wr_ref[...], preferred_element_type=F32) + br_ref[...]
    lane = lax.broadcasted_iota(jnp.int32, logits.shape, 1)
    big = jnp.int32(1 << 20)
    neg = -jnp.inf
    lg = jnp.where(lane < N_GROUPS, logits, neg)
    gmax = jnp.max(lg, axis=-1, keepdims=True)
    g_val = 1.0 / jnp.sum(jnp.exp(lg - gmax), axis=-1, keepdims=True)
    g_idx = jnp.min(jnp.where(lg == gmax, lane, big), axis=-1, keepdims=True)
    e_lo = N_GROUPS + g_idx * E_PER_G
    emask = (lane >= e_lo) & (lane < e_lo + E_PER_G)
    le = jnp.where(emask, logits, neg)
    m1 = jnp.max(le, axis=-1, keepdims=True)
    i1 = jnp.min(jnp.where(le == m1, lane, big), axis=-1, keepdims=True)
    le2 = jnp.where(lane == i1, neg, le)
    m2 = jnp.max(le2, axis=-1, keepdims=True)
    i2 = jnp.min(jnp.where(le2 == m2, lane, big), axis=-1, keepdims=True)
    z_e = jnp.sum(jnp.exp(le - m1), axis=-1, keepdims=True)
    p1 = 1.0 / z_e
    p2 = jnp.exp(m2 - m1) / z_e
    w1 = g_val * (p1 / (p1 + p2))
    w2 = g_val * (p2 / (p1 + p2))
    comb = jnp.where(lane == i1, w1, 0.0) + jnp.where(lane == i2, w2, 0.0)
    onehot = jnp.where(lane == g_idx, 1.0, 0.0)
    onehot_t = onehot.T[0:2 * SUBLANE, :]
    rank_row = jnp.dot(onehot_t.astype(BF16), triu_ref[...], preferred_element_type=F32)
    rrow_ref[0] = jnp.where(onehot_t > 0.0, rank_row, -1.0)[0:SUBLANE, :]
    cnt_ref[0] = jnp.broadcast_to(jnp.sum(onehot, axis=0, keepdims=True), (SUBLANE, LANE))
    for g in range(N_GROUPS):
        shifted = pltpu.roll(comb, LANE - (N_GROUPS + g * E_PER_G), axis=1)
        comb_ref[:, g * LANE:(g + 1) * LANE] = jnp.where(lane < E_PER_G, shifted, 0.0)


def _mix_call(attn2d, uc2d, gates, x2d, ada_l, woa, wco, wout, g2, wr, br, S):
    T, D = x2d.shape
    tm = TM_MIX
    tpb = S // tm
    triu = jnp.tri(tm, tm, -1, dtype=BF16).T
    const = lambda shape: pl.BlockSpec(shape, lambda i: (0,) * len(shape))
    row = lambda w: pl.BlockSpec((tm, w), lambda i: (i, 0))
    return pl.pallas_call(
        _mix_kernel,
        grid=(T // tm,),
        in_specs=[
            row(N_HEADS * V_DIM),
            row(CONV_C),
            row(2 * D),
            row(D),
            pl.BlockSpec((1, 6, D), lambda i: (i // tpb, 0, 0)),
            const((N_HEADS * V_DIM, D)),
            const((CONV_C, D)),
            const((D, D)),
            const((1, D)),
            const((D, LANE)),
            const((1, LANE)),
            const((tm, tm)),
        ],
        out_specs=[row(D), row(D), row(N_GROUPS * LANE),
                   pl.BlockSpec((1, SUBLANE, tm), lambda i: (i, 0, 0)),
                   pl.BlockSpec((1, SUBLANE, LANE), lambda i: (i, 0, 0))],
        out_shape=[jax.ShapeDtypeStruct((T, D), F32),
                   jax.ShapeDtypeStruct((T, D), BF16),
                   jax.ShapeDtypeStruct((T, N_GROUPS * LANE), F32),
                   jax.ShapeDtypeStruct((T // tm, SUBLANE, tm), F32),
                   jax.ShapeDtypeStruct((T // tm, SUBLANE, LANE), F32)],
        compiler_params=_cparams(("arbitrary",)),
        name="mixer_merge",
    )(attn2d, uc2d, gates, x2d, ada_l, woa, wco, wout, g2, wr, br, triu)


def _moe_kernel(cnt_ref, h_ref, comb_ref, rrow_ref, x1_ref, ada_ref, wg_ref, wu_ref, wd_ref, gf_ref,
                o_ref, acc_ref, *, final_norm):
    i = pl.program_id(0)
    g = pl.program_id(1)
    tm = h_ref.shape[0]

    @pl.when(g == 0)
    def _():
        acc_ref[...] = jnp.zeros_like(acc_ref)

    rank_row = rrow_ref[0, pl.ds(g, 1), :]
    row_id = lax.broadcasted_iota(jnp.int32, (MOE_CHUNK, tm), 0).astype(F32)
    cw = comb_ref[...]
    cw_hi = cw.astype(BF16)
    cw_lo = (cw - cw_hi.astype(F32)).astype(BF16)
    cw2 = jnp.concatenate([cw_hi, cw_lo], axis=1)

    def chunk(c, carry):
        r0 = (c * MOE_CHUNK).astype(F32)
        sel = jnp.where(row_id + r0 == rank_row, 1.0, 0.0).astype(BF16)
        hc = jnp.dot(sel, h_ref[...], preferred_element_type=F32).astype(BF16)
        cwc = jnp.dot(sel, cw2, preferred_element_type=F32)
        cwc = cwc[:, :LANE] + cwc[:, LANE:]
        a = jnp.dot(hc, wg_ref[g], preferred_element_type=F32)
        u = jnp.dot(hc, wu_ref[g], preferred_element_type=F32)
        parts = []
        for e in range(E_PER_G):
            sl = slice(e * D_EXPERT, (e + 1) * D_EXPERT)
            ae = a[:, sl]
            ce = jnp.broadcast_to(cwc[:, e:e + 1], (MOE_CHUNK, D_EXPERT))
            parts.append((ae * jax.nn.sigmoid(ae) * u[:, sl] * ce).astype(BF16))
        y = jnp.dot(jnp.concatenate(parts, axis=-1), wd_ref[g], preferred_element_type=F32)
        acc_ref[...] += lax.dot_general(sel, y.astype(BF16), (((0,), (0,)), ((), ())),
                                        preferred_element_type=F32)
        return carry

    n = cnt_ref[i, g]
    lax.fori_loop(0, lax.shift_right_logical(n + (MOE_CHUNK - 1), MOE_CHUNK.bit_length() - 1), chunk, 0)

    @pl.when(g == N_GROUPS - 1)
    def _():
        gt2 = ada_ref[0, 5:6, :]
        x2 = x1_ref[...] + gt2 * acc_ref[...]
        if final_norm:
            x2 = _rms(x2, gf_ref[...])
        o_ref[...] = x2


def _moe_call(cnt, h2, comb, rrow, x1, ada_l, wg, wu, wd, gf, S, final_norm):
    T, D = x1.shape
    tm = TM_MIX
    tpb = S // tm
    EF = E_PER_G * D_EXPERT
    assert MOE_CHUNK & (MOE_CHUNK - 1) == 0
    resident = lambda shape: pl.BlockSpec(shape, lambda i, g, c: (0,) * len(shape),
                                          pipeline_mode=pl.Buffered(1))
    return pl.pallas_call(
        functools.partial(_moe_kernel, final_norm=final_norm),
        grid_spec=pltpu.PrefetchScalarGridSpec(
            num_scalar_prefetch=1,
            grid=(T // tm, N_GROUPS),
            in_specs=[
                pl.BlockSpec((tm, D), lambda i, g, c: (i, 0)),
                pl.BlockSpec((tm, LANE), lambda i, g, c: (i, g)),
                pl.BlockSpec((1, SUBLANE, tm), lambda i, g, c: (i, 0, 0)),
                pl.BlockSpec((tm, D), lambda i, g, c: (i, 0)),
                pl.BlockSpec((1, 6, D), lambda i, g, c: (i // tpb, 0, 0)),
                resident((N_GROUPS, D, EF)),
                resident((N_GROUPS, D, EF)),
                resident((N_GROUPS, EF, D)),
                pl.BlockSpec((1, D), lambda i, g, c: (0, 0)),
            ],
            out_specs=pl.BlockSpec((tm, D), lambda i, g, c: (i, 0)),
            scratch_shapes=[pltpu.VMEM((tm, D), F32)],
        ),
        out_shape=jax.ShapeDtypeStruct((T, D), F32),
        compiler_params=_cparams(("arbitrary", "arbitrary")),
        name="moe_experts",
    )(cnt, h2, comb, rrow, x1, ada_l, wg, wu, wd, gf)


def _pad_heads(w, width):
    k = w.shape[0]
    w = w.reshape(k, N_HEADS, width)
    w = jnp.pad(w, ((0, 0), (0, 0), (0, HEAD_PAD - width)))
    return w.reshape(k, N_HEADS * HEAD_PAD)


def _layout_w_in(w):
    d = w.shape[0]
    sp_kr = Q_LORA + KV_LORA
    sp_conv = sp_kr + QK_ROPE
    zeros = lambda n: jnp.zeros((d, n), w.dtype)
    return jnp.concatenate(
        [w[:, :sp_kr], zeros(KPE_OFF), w[:, sp_kr:sp_conv], zeros(LANE - KPE_OFF - QK_ROPE), w[:, sp_conv:]],
        axis=1)


def kernel(x, c, positions, ada_w, ada_b, norm1_g, norm2_g, w_in, q_norm_g, w_uq, kv_norm_g, w_ukv,
           w_o_attn, conv_w, conv_b, conv_ln_g, conv_ln_b, w_conv_out, w_out, router_group_w,
           router_group_b, router_expert_w, router_expert_b, expert_w_gate, expert_w_up,
           expert_w_down, final_norm_g):
    B, S, D = x.shape
    L = ada_w.shape[0]
    T = B * S

    c_pad = jnp.pad(c, ((0, 8 - B), (0, 0)))
    ada = _ada_call(c_pad, ada_w, ada_b)[:, :B].reshape(L, B, 6, D)

    inv_freq = ROPE_BASE ** (-jnp.arange(0, QK_ROPE, 2, dtype=F32) / QK_ROPE)
    freq_row = jnp.zeros((1, LANE), F32)
    freq_row = freq_row.at[0, KPE_OFF:KPE_OFF + QK_ROPE // 2].set(inv_freq)
    freq_row = freq_row.at[0, KPE_OFF + QK_ROPE // 2:KPE_OFF + QK_ROPE].set(inv_freq)
    tc, ts1, ts2 = _rope_call(positions.reshape(T, 1), freq_row)

    x2d = x.reshape(T, D)
    for l in range(L):
        w_in_p = _layout_w_in(w_in[l]).astype(BF16)
        w_uq_p = _pad_heads(w_uq[l], QK_DIM).astype(BF16)
        wkv = w_ukv[l].reshape(KV_LORA, N_HEADS, QK_NOPE + V_DIM)
        w_uk_p = _pad_heads(wkv[:, :, :QK_NOPE].reshape(KV_LORA, -1), QK_NOPE).astype(BF16)
        w_uvt = jnp.pad(wkv[:, :, QK_NOPE:], ((0, 0), (0, 0), (0, VT_ROWS - V_DIM)))
        w_uvt_p = w_uvt.reshape(KV_LORA, N_HEADS * VT_ROWS).T.astype(BF16)
        cw = jnp.pad(conv_w[l], ((0, HALO - CONV_K), (0, 0)))
        q, k, vt, uc, gates = _inproj_call(
            x2d, ada[l], norm1_g[l].reshape(1, D), w_in_p, q_norm_g[l].reshape(1, -1), w_uq_p,
            kv_norm_g[l].reshape(1, -1), w_uk_p, w_uvt_p, tc, ts1, ts2,
            cw, conv_b[l].reshape(1, -1), conv_ln_g[l].reshape(1, -1), conv_ln_b[l].reshape(1, -1), B, S)

        attn = _attn_call(q, k, vt).reshape(T, N_HEADS * V_DIM)

        w_r = jnp.concatenate([router_group_w[l], router_expert_w[l]], axis=1)
        w_r = jnp.pad(w_r, ((0, 0), (0, LANE - w_r.shape[1]))).astype(BF16)
        b_r = jnp.concatenate([router_group_b[l], router_expert_b[l]])
        b_r = jnp.pad(b_r, (0, LANE - b_r.shape[0])).reshape(1, LANE)
        x1, h2, comb, rrow, cnt = _mix_call(
            attn, uc, gates, x2d, ada[l], w_o_attn[l].astype(BF16), w_conv_out[l].astype(BF16),
            w_out[l].astype(BF16), norm2_g[l].reshape(1, D), w_r, b_r, S)

        EF = E_PER_G * D_EXPERT
        wg = expert_w_gate[l].transpose(0, 2, 1, 3).reshape(N_GROUPS, D, EF).astype(BF16)
        wu = expert_w_up[l].transpose(0, 2, 1, 3).reshape(N_GROUPS, D, EF).astype(BF16)
        wd = expert_w_down[l].reshape(N_GROUPS, EF, D).astype(BF16)
        cnt_i = cnt[:, 0, :N_GROUPS].astype(jnp.int32)
        x2d = _moe_call(cnt_i, h2, comb, rrow, x1, ada[l], wg, wu, wd, final_norm_g.reshape(1, D), S,
                        final_norm=(l == L - 1))
    return x2d.reshape(B, S, D)
```

```python
import functools
import math

import jax
import jax.numpy as jnp
from jax import lax
from jax.experimental import pallas as pl
from jax.experimental.pallas import tpu as pltpu

D_MODEL = 1024
N_HEADS = 8
QK_NOPE = 64
QK_ROPE = 32
V_DIM = 64
QK_DIM = QK_NOPE + QK_ROPE
Q_LORA = 384
KV_LORA = 256
ROPE_BASE = 10000.0
CONV_C = 512
CONV_K = 31
N_GROUPS = 4
E_PER_G = 8
D_EXPERT = 128
EPS = 1e-6

LANE = 128
SUBLANE = 8
CONV_TCHUNK = 128
HEAD_PAD = LANE
KPE_OFF = QK_NOPE
VT_ROWS = 80
HALO = 32
VMEM_LIMIT = 56 * 1024 * 1024

C_QLAT = 0
C_KVLAT = Q_LORA
C_KPE = Q_LORA + KV_LORA
C_GLU_A = C_KPE + LANE
C_GLU_B = C_GLU_A + CONV_C
C_GATE = C_GLU_B + CONV_C
IN_COLS_PAD = C_GATE + 2 * D_MODEL

TM_IN = 512
TM_MIX = 512
MOE_CHUNK = 160
TQ = 512
TK = 256

BF16 = jnp.bfloat16
F32 = jnp.float32


def _cparams(sem, flags=None):
    return pltpu.CompilerParams(dimension_semantics=sem, vmem_limit_bytes=VMEM_LIMIT, flags=flags)


def _ada_kernel(c_ref, w_ref, b_ref, o_ref):
    c = c_ref[...]
    c_act = c * jax.nn.sigmoid(c)
    o_ref[0] = jnp.dot(c_act, w_ref[0], preferred_element_type=F32,
                       precision=lax.Precision.HIGHEST) + b_ref[0]


def _ada_call(c_pad, ada_w, ada_b):
    L, D, N = ada_w.shape
    tn = 1536
    return pl.pallas_call(
        _ada_kernel,
        grid=(L, N // tn),
        in_specs=[
            pl.BlockSpec((8, D), lambda l, j: (0, 0)),
            pl.BlockSpec((1, D, tn), lambda l, j: (l, 0, j)),
            pl.BlockSpec((1, 1, tn), lambda l, j: (l, 0, j)),
        ],
        out_specs=pl.BlockSpec((1, 8, tn), lambda l, j: (l, 0, j)),
        out_shape=jax.ShapeDtypeStruct((L, 8, N), F32),
        compiler_params=_cparams(("arbitrary", "arbitrary")),
        name="ada_ln",
    )(c_pad, ada_w, ada_b.reshape(L, 1, N))


def _rope_kernel(pos_ref, freq_ref, c_ref, s1_ref, s2_ref):
    pos = pos_ref[...].astype(F32)
    ang = pos * freq_ref[...]
    lane = lax.broadcasted_iota(jnp.int32, ang.shape, 1)
    cosv = jnp.cos(ang)
    sinv = jnp.sin(ang)
    is_x1 = (lane >= KPE_OFF) & (lane < KPE_OFF + QK_ROPE // 2)
    is_x2 = (lane >= KPE_OFF + QK_ROPE // 2) & (lane < KPE_OFF + QK_ROPE)
    c_ref[...] = jnp.where(is_x1 | is_x2, cosv, jnp.where(lane < KPE_OFF, 1.0, 0.0))
    s1_ref[...] = jnp.where(is_x2, sinv, 0.0)
    s2_ref[...] = jnp.where(is_x1, -sinv, 0.0)


def _rope_call(pos_col, freq_row):
    T = pos_col.shape[0]
    tm = 2048
    spec = pl.BlockSpec((tm, LANE), lambda i: (i, 0))
    return pl.pallas_call(
        _rope_kernel,
        grid=(T // tm,),
        in_specs=[pl.BlockSpec((tm, 1), lambda i: (i, 0)),
                  pl.BlockSpec((1, LANE), lambda i: (0, 0))],
        out_specs=[spec, spec, spec],
        out_shape=[jax.ShapeDtypeStruct((T, LANE), F32)] * 3,
        compiler_params=_cparams(("arbitrary",)),
        name="rope_tables",
    )(pos_col, freq_row)


def _rms(x, g):
    ms = jnp.mean(x * x, axis=-1, keepdims=True)
    return x * lax.rsqrt(ms + EPS) * g


def _rope_apply(x, c, s1, s2):
    half = QK_ROPE // 2
    return x * c + pltpu.roll(x, half, axis=1) * s1 + pltpu.roll(x, LANE - half, axis=1) * s2


def _conv_ln_silu(buf_ref, cw_ref, cb_ref, lng_ref, lnb_ref, tm):
    base = HALO - (CONV_K - 1)
    wrows = CONV_TCHUNK + HALO
    pieces = []
    for t0 in range(0, tm, CONV_TCHUNK):
        row = []
        for c0 in range(0, CONV_C, LANE):
            cs = slice(c0, c0 + LANE)
            win = buf_ref[t0:t0 + wrows, cs]
            acc = jnp.broadcast_to(cb_ref[:, cs], (CONV_TCHUNK, LANE))
            for r in range(SUBLANE):
                shifted = win if r == 0 else pltpu.roll(win, wrows - r, axis=0)
                for o in range(base, base + CONV_K):
                    if o % SUBLANE == r:
                        acc = acc + shifted[o - r:o - r + CONV_TCHUNK] * cw_ref[o - base:o - base + 1, cs]
            row.append(acc)
        pieces.append(jnp.concatenate(row, axis=1))
    acc = jnp.concatenate(pieces, axis=0)
    mu = jnp.mean(acc, axis=-1, keepdims=True)
    xc = acc - mu
    var = jnp.mean(xc * xc, axis=-1, keepdims=True)
    ln = xc * lax.rsqrt(var + EPS) * lng_ref[...] + lnb_ref[...]
    return ln * jax.nn.sigmoid(ln)


def _inproj_kernel(x_ref, ada_ref, g1_ref, win_ref, qg_ref, wuq_ref, kvg_ref, wuk_ref, wuvt_ref,
                   c_ref, s1_ref, s2_ref, cw_ref, cb_ref, lng_ref, lnb_ref,
                   q_ref, k_ref, vt_ref, uc_ref, gate_ref, buf_ref, *, tiles_per_batch):
    tm = x_ref.shape[0]
    first = (pl.program_id(0) % tiles_per_batch) == 0

    @pl.when(first)
    def _():
        buf_ref[0:HALO, :] = jnp.zeros((HALO, CONV_C), F32)

    @pl.when(jnp.logical_not(first))
    def _():
        buf_ref[0:HALO, :] = buf_ref[tm:tm + HALO, :]

    x = x_ref[...]
    sh1 = ada_ref[0, 0:1, :]
    sc1 = ada_ref[0, 1:2, :]
    h = _rms(x, g1_ref[...]) * (1.0 + sc1) + sh1
    z = jnp.dot(h.astype(BF16), win_ref[...], preferred_element_type=F32)

    qn = _rms(z[:, C_QLAT:C_QLAT + Q_LORA], qg_ref[...]).astype(BF16)
    kvn = _rms(z[:, C_KVLAT:C_KVLAT + KV_LORA], kvg_ref[...]).astype(BF16)
    q = jnp.dot(qn, wuq_ref[...], preferred_element_type=F32)
    kk = jnp.dot(kvn, wuk_ref[...], preferred_element_type=F32)
    vvt = lax.dot_general(wuvt_ref[...], kvn, (((1,), (1,)), ((), ())), preferred_element_type=F32)
    vrow = lax.broadcasted_iota(jnp.int32, (VT_ROWS, 1), 0)
    ones_row = jnp.where(vrow == V_DIM, 1.0, 0.0)

    c = c_ref[...]
    s1 = s1_ref[...]
    s2 = s2_ref[...]
    qs = (QK_DIM ** -0.5) * math.log2(math.e)
    cq, s1q, s2q = c * qs, s1 * qs, s2 * qs
    kpe = _rope_apply(z[:, C_KPE:C_KPE + LANE], c, s1, s2)
    for hh in range(N_HEADS):
        sl = slice(hh * HEAD_PAD, (hh + 1) * HEAD_PAD)
        q_ref[0, hh] = _rope_apply(q[:, sl], cq, s1q, s2q).astype(BF16)
        k_ref[0, hh] = (kk[:, sl] + kpe).astype(BF16)
        vth = (vvt[hh * VT_ROWS:(hh + 1) * VT_ROWS, :] + ones_row).astype(BF16)
        for j in range(vt_ref.shape[2]):
            vt_ref[0, hh, j] = vth[:, j * TK:(j + 1) * TK]

    gate_ref[...] = jax.nn.sigmoid(z[:, C_GATE:C_GATE + 2 * D_MODEL]).astype(BF16)

    buf_ref[HALO:HALO + tm, :] = z[:, C_GLU_A:C_GLU_A + CONV_C] * jax.nn.sigmoid(z[:, C_GLU_B:C_GLU_B + CONV_C])
    uc_ref[...] = _conv_ln_silu(buf_ref, cw_ref, cb_ref, lng_ref, lnb_ref, tm).astype(BF16)


def _inproj_call(x2d, ada_l, g1, w_in_p, qg, w_uq_p, kvg, w_uk_p, w_uvt_p, tc, ts1, ts2,
                 cw, cb, lng, lnb, B, S):
    T, D = x2d.shape
    tm = TM_IN
    tpb = S // tm
    kpt = tm // TK
    const = lambda shape: pl.BlockSpec(shape, lambda i: (0,) * len(shape))
    head_spec = pl.BlockSpec((1, N_HEADS, tm, HEAD_PAD), lambda i: (i // tpb, 0, i % tpb, 0))
    head_shape = jax.ShapeDtypeStruct((B, N_HEADS, S, HEAD_PAD), BF16)
    tab = pl.BlockSpec((tm, LANE), lambda i: (i, 0))
    return pl.pallas_call(
        functools.partial(_inproj_kernel, tiles_per_batch=tpb),
        grid=(T // tm,),
        in_specs=[
            pl.BlockSpec((tm, D), lambda i: (i, 0)),
            pl.BlockSpec((1, 6, D), lambda i: (i // tpb, 0, 0)),
            const((1, D)),
            const((D, IN_COLS_PAD)),
            const((1, Q_LORA)),
            const((Q_LORA, N_HEADS * HEAD_PAD)),
            const((1, KV_LORA)),
            const((KV_LORA, N_HEADS * HEAD_PAD)),
            const((N_HEADS * VT_ROWS, KV_LORA)),
            tab, tab, tab,
            const((HALO, CONV_C)), const((1, CONV_C)), const((1, CONV_C)), const((1, CONV_C)),
        ],
        out_specs=[head_spec, head_spec,
                   pl.BlockSpec((1, N_HEADS, kpt, VT_ROWS, TK), lambda i: (i // tpb, 0, i % tpb, 0, 0)),
                   pl.BlockSpec((tm, CONV_C), lambda i: (i, 0)),
                   pl.BlockSpec((tm, 2 * D_MODEL), lambda i: (i, 0))],
        out_shape=[head_shape, head_shape,
                   jax.ShapeDtypeStruct((B, N_HEADS, S // TK, VT_ROWS, TK), BF16),
                   jax.ShapeDtypeStruct((T, CONV_C), BF16),
                   jax.ShapeDtypeStruct((T, 2 * D_MODEL), BF16)],
        scratch_shapes=[pltpu.VMEM((HALO + tm, CONV_C), F32)],
        compiler_params=_cparams(("arbitrary",)),
        name="in_proj",
    )(x2d, ada_l, g1, w_in_p, qg, w_uq_p, kvg, w_uk_p, w_uvt_p, tc, ts1, ts2, cw, cb, lng, lnb)


HEADS_PER_STEP = 8


def _attn_kernel(q_ref, k_ref, vt_ref, o_ref, *scratch):
    qi = pl.program_id(2)
    heads = range(HEADS_PER_STEP)
    assert TQ == 2 * TK
    hs = HEADS_PER_STEP
    s_ref = [scratch[2 * hh:2 * hh + 2] for hh in heads]
    p_ref = [scratch[2 * hs + 2 * hh:2 * hs + 2 * hh + 2] for hh in heads]
    acc_ref = scratch[4 * hs:5 * hs]

    def qk(kb, slot, qlo=0):
        for hh in heads:
            k = k_ref[0, hh, pl.ds(kb * TK, TK), :]
            s_ref[hh][slot][:, qlo:] = lax.dot_general(k, q_ref[0, hh, qlo:, :], (((1,), (1,)), ((), ())),
                                                       preferred_element_type=F32)

    def pv(kb, slot, alpha, qlo=0):
        for hh in heads:
            acc_ref[hh][:, qlo:] = (acc_ref[hh][:, qlo:] * alpha[hh]
                                    + jnp.dot(vt_ref[0, hh, kb], p_ref[hh][slot][:, qlo:],
                                              preferred_element_type=F32))

    def softmax(slot, m, mask=None, qlo=0):
        m_out, alpha = [], []
        for hh in heads:
            s = s_ref[hh][slot][:, qlo:]
            if mask is not None:
                s = jnp.where(mask[:, qlo:], s, -jnp.inf)
            m_old = m[hh][:, qlo:]
            m_new = jnp.maximum(m_old, jnp.max(s, axis=0, keepdims=True))
            p_ref[hh][slot][:, qlo:] = jnp.exp2(s - m_new).astype(BF16)
            alpha.append(jnp.exp2(m_old - m_new))
            m_out.append(m_new)
        return tuple(m_out), tuple(alpha)

    for hh in heads:
        acc_ref[hh][...] = jnp.zeros((VT_ROWS, TQ), F32)
        p_ref[hh][1][...] = jnp.zeros((TK, TQ), BF16)
    m0 = tuple(jnp.full((1, TQ), -jnp.inf, F32) for _ in heads)
    one = tuple(jnp.ones((1, TQ), F32) for _ in heads)
    qk(0, 0)

    def pair(a, carry):
        m, alpha1 = carry
        qk(a + 1, 1)
        pv(jnp.maximum(a - 1, 0), 1, alpha1)
        m, alpha0 = softmax(0, m)
        qk(a + 2, 0)
        pv(a, 0, alpha0)
        m, alpha1 = softmax(1, m)
        return m, alpha1

    m, alpha1 = lax.fori_loop(0, qi, lambda i, c: pair(2 * i, c), (m0, one))
    a = 2 * qi
    key = lax.broadcasted_iota(jnp.int32, (TK, TQ), 0)
    qry = lax.broadcasted_iota(jnp.int32, (TK, TQ), 1)
    qk(a + 1, 1, qlo=TK)
    pv(jnp.maximum(a - 1, 0), 1, alpha1)
    m, alpha0 = softmax(0, m, key <= qry)
    pv(a, 0, alpha0)
    _, alpha1 = softmax(1, m, key + TK <= qry, qlo=TK)
    pv(a + 1, 1, alpha1, qlo=TK)
    outs = []
    for hh in heads:
        acc = acc_ref[hh][...]
        o_t = acc[:V_DIM, :] / acc[V_DIM:V_DIM + 1, :]
        outs.append(o_t.T)
    o_ref[0] = jnp.concatenate(outs, axis=-1).astype(BF16)


def _attn_call(q, k, vt):
    B, H, S, _ = q.shape
    nq = S // TQ
    hs = HEADS_PER_STEP
    return pl.pallas_call(
        _attn_kernel,
        grid=(B, H // hs, nq),
        in_specs=[
            pl.BlockSpec((1, hs, TQ, HEAD_PAD), lambda b, h, i: (b, h, i, 0)),
            pl.BlockSpec((1, hs, S, HEAD_PAD), lambda b, h, i: (b, h, 0, 0),
                         pipeline_mode=pl.Buffered(1)),
            pl.BlockSpec((1, hs, S // TK, VT_ROWS, TK), lambda b, h, i: (b, h, 0, 0, 0),
                         pipeline_mode=pl.Buffered(1)),
        ],
        out_specs=pl.BlockSpec((1, TQ, hs * V_DIM), lambda b, h, i: (b, i, h)),
        out_shape=jax.ShapeDtypeStruct((B, S, H * V_DIM), BF16),
        scratch_shapes=([pltpu.VMEM((TK, TQ), F32)] * (2 * hs) + [pltpu.VMEM((TK, TQ), BF16)] * (2 * hs)
                        + [pltpu.VMEM((VT_ROWS, TQ), F32)] * hs),
        compiler_params=_cparams(("arbitrary", "arbitrary", "arbitrary")),
        name="mla_attention",
    )(q, k, vt)


def _mix_kernel(attn_ref, uc_ref, gate_ref, x_ref, ada_ref, woa_ref, wco_ref, wout_ref, g2_ref,
                wr_ref, br_ref, triu_ref, x1_ref, h2_ref, comb_ref, rrow_ref, cnt_ref):
    y_conv = jnp.dot(uc_ref[...], wco_ref[...], preferred_element_type=F32)
    y_attn = jnp.dot(attn_ref[...], woa_ref[...], preferred_element_type=F32)

    g_attn = gate_ref[:, 0:D_MODEL].astype(F32)
    g_conv = gate_ref[:, D_MODEL:2 * D_MODEL].astype(F32)
    merged = g_attn * y_attn + g_conv * y_conv
    y = jnp.dot(merged.astype(BF16), wout_ref[...], preferred_element_type=F32)
    gt1 = ada_ref[0, 2:3, :]
    x1 = x_ref[...] + gt1 * y
    x1_ref[...] = x1

    sh2 = ada_ref[0, 3:4, :]
    sc2 = ada_ref[0, 4:5, :]
    h2 = (_rms(x1, g2_ref[...]) * (1.0 + sc2) + sh2).astype(BF16)
    h2_ref[...] = h2

    logits = jnp.dot(h2, wr_ref[...], preferred_element_type=F32) + br_ref[...]
    lane = lax.broadcasted_iota(jnp.int32, logits.shape, 1)
    big = jnp.int32(1 << 20)
    neg = -jnp.inf
    lg = jnp.where(lane < N_GROUPS, logits, neg)
    gmax = jnp.max(lg, axis=-1, keepdims=True)
    g_val = 1.0 / jnp.sum(jnp.exp(lg - gmax), axis=-1, keepdims=True)
    g_idx = jnp.min(jnp.where(lg == gmax, lane, big), axis=-1, keepdims=True)
    e_lo = N_GROUPS + g_idx * E_PER_G
    emask = (lane >= e_lo) & (lane < e_lo + E_PER_G)
    le = jnp.where(emask, logits, neg)
    m1 = jnp.max(le, axis=-1, keepdims=True)
    i1 = jnp.min(jnp.where(le == m1, lane, big), axis=-1, keepdims=True)
    le2 = jnp.where(lane == i1, neg, le)
    m2 = jnp.max(le2, axis=-1, keepdims=True)
    i2 = jnp.min(jnp.where(le2 == m2, lane, big), axis=-1, keepdims=True)
    z_e = jnp.sum(jnp.exp(le - m1), axis=-1, keepdims=True)
    p1 = 1.0 / z_e
    p2 = jnp.exp(m2 - m1) / z_e
    w1 = g_val * (p1 / (p1 + p2))
    w2 = g_val * (p2 / (p1 + p2))
    comb = jnp.where(lane == i1, w1, 0.0) + jnp.where(lane == i2, w2, 0.0)
    onehot = jnp.where(lane == g_idx, 1.0, 0.0)
    onehot_t = onehot.T[0:2 * SUBLANE, :]
    rank_row = jnp.dot(onehot_t.astype(BF16), triu_ref[...], preferred_element_type=F32)
    rrow_ref[0] = jnp.where(onehot_t > 0.0, rank_row, -1.0)[0:SUBLANE, :]
    cnt_ref[0] = jnp.broadcast_to(jnp.sum(onehot, axis=0, keepdims=True), (SUBLANE, LANE))
    for g in range(N_GROUPS):
        shifted = pltpu.roll(comb, LANE - (N_GROUPS + g * E_PER_G), axis=1)
        comb_ref[:, g * LANE:(g + 1) * LANE] = jnp.where(lane < E_PER_G, shifted, 0.0)


def _mix_call(attn2d, uc2d, gates, x2d, ada_l, woa, wco, wout, g2, wr, br, S):
    T, D = x2d.shape
    tm = TM_MIX
    tpb = S // tm
    triu = jnp.tri(tm, tm, -1, dtype=BF16).T
    const = lambda shape: pl.BlockSpec(shape, lambda i: (0,) * len(shape))
    row = lambda w: pl.BlockSpec((tm, w), lambda i: (i, 0))
    return pl.pallas_call(
        _mix_kernel,
        grid=(T // tm,),
        in_specs=[
            row(N_HEADS * V_DIM),
            row(CONV_C),
            row(2 * D),
            row(D),
            pl.BlockSpec((1, 6, D), lambda i: (i // tpb, 0, 0)),
            const((N_HEADS * V_DIM, D)),
            const((CONV_C, D)),
            const((D, D)),
            const((1, D)),
            const((D, LANE)),
            const((1, LANE)),
            const((tm, tm)),
        ],
        out_specs=[row(D), row(D), row(N_GROUPS * LANE),
                   pl.BlockSpec((1, SUBLANE, tm), lambda i: (i, 0, 0)),
                   pl.BlockSpec((1, SUBLANE, LANE), lambda i: (i, 0, 0))],
        out_shape=[jax.ShapeDtypeStruct((T, D), F32),
                   jax.ShapeDtypeStruct((T, D), BF16),
                   jax.ShapeDtypeStruct((T, N_GROUPS * LANE), F32),
                   jax.ShapeDtypeStruct((T // tm, SUBLANE, tm), F32),
                   jax.ShapeDtypeStruct((T // tm, SUBLANE, LANE), F32)],
        compiler_params=_cparams(("arbitrary",)),
        name="mixer_merge",
    )(attn2d, uc2d, gates, x2d, ada_l, woa, wco, wout, g2, wr, br, triu)


def _moe_kernel(cnt_ref, h_ref, comb_ref, rrow_ref, x1_ref, ada_ref, wg_ref, wu_ref, wd_ref, gf_ref,
                o_ref, acc_ref, *, final_norm):
    i = pl.program_id(0)
    tm = h_ref.shape[0]
    acc_ref[...] = jnp.zeros_like(acc_ref)
    row_id = lax.broadcasted_iota(jnp.int32, (MOE_CHUNK, tm), 0).astype(F32)

    for g in range(N_GROUPS):
        rank_row = rrow_ref[0, g:g + 1, :]
        cw = comb_ref[:, g * LANE:(g + 1) * LANE]
        cw_hi = cw.astype(BF16)
        cw_lo = (cw - cw_hi.astype(F32)).astype(BF16)
        cw2 = jnp.concatenate([cw_hi, cw_lo], axis=1)

        def chunk(c, carry, g=g, rank_row=rank_row, cw2=cw2):
            r0 = (c * MOE_CHUNK).astype(F32)
            sel = jnp.where(row_id + r0 == rank_row, 1.0, 0.0).astype(BF16)
            hc = jnp.dot(sel, h_ref[...], preferred_element_type=F32).astype(BF16)
            cwc = jnp.dot(sel, cw2, preferred_element_type=F32)
            cwc = cwc[:, :LANE] + cwc[:, LANE:]
            a = jnp.dot(hc, wg_ref[g], preferred_element_type=F32)
            u = jnp.dot(hc, wu_ref[g], preferred_element_type=F32)
            parts = []
            for e in range(E_PER_G):
                sl = slice(e * D_EXPERT, (e + 1) * D_EXPERT)
                ae = a[:, sl]
                ce = jnp.broadcast_to(cwc[:, e:e + 1], (MOE_CHUNK, D_EXPERT))
                parts.append((ae * jax.nn.sigmoid(ae) * u[:, sl] * ce).astype(BF16))
            y = jnp.dot(jnp.concatenate(parts, axis=-1), wd_ref[g], preferred_element_type=F32)
            acc_ref[...] += lax.dot_general(sel, y.astype(BF16), (((0,), (0,)), ((), ())),
                                            preferred_element_type=F32)
            return carry

        n = cnt_ref[i, g]
        n_chunks = sum((n > k * MOE_CHUNK).astype(jnp.int32) for k in range(pl.cdiv(tm, MOE_CHUNK)))
        lax.fori_loop(0, n_chunks, chunk, 0)

    gt2 = ada_ref[0, 5:6, :]
    x2 = x1_ref[...] + gt2 * acc_ref[...]
    if final_norm:
        x2 = _rms(x2, gf_ref[...])
    o_ref[...] = x2


def _moe_call(cnt, h2, comb, rrow, x1, ada_l, wg, wu, wd, gf, S, final_norm):
    T, D = x1.shape
    tm = TM_MIX
    tpb = S // tm
    EF = E_PER_G * D_EXPERT
    resident = lambda shape: pl.BlockSpec(shape, lambda i, c: (0,) * len(shape),
                                          pipeline_mode=pl.Buffered(1))
    return pl.pallas_call(
        functools.partial(_moe_kernel, final_norm=final_norm),
        grid_spec=pltpu.PrefetchScalarGridSpec(
            num_scalar_prefetch=1,
            grid=(T // tm,),
            in_specs=[
                pl.BlockSpec((tm, D), lambda i, c: (i, 0)),
                pl.BlockSpec((tm, N_GROUPS * LANE), lambda i, c: (i, 0)),
                pl.BlockSpec((1, SUBLANE, tm), lambda i, c: (i, 0, 0)),
                pl.BlockSpec((tm, D), lambda i, c: (i, 0)),
                pl.BlockSpec((1, 6, D), lambda i, c: (i // tpb, 0, 0)),
                resident((N_GROUPS, D, EF)),
                resident((N_GROUPS, D, EF)),
                resident((N_GROUPS, EF, D)),
                pl.BlockSpec((1, D), lambda i, c: (0, 0)),
            ],
            out_specs=pl.BlockSpec((tm, D), lambda i, c: (i, 0)),
            scratch_shapes=[pltpu.VMEM((tm, D), F32)],
        ),
        out_shape=jax.ShapeDtypeStruct((T, D), F32),
        compiler_params=_cparams(("arbitrary",)),
        name="moe_experts",
    )(cnt, h2, comb, rrow, x1, ada_l, wg, wu, wd, gf)


def _pad_heads(w, width):
    k = w.shape[0]
    w = w.reshape(k, N_HEADS, width)
    w = jnp.pad(w, ((0, 0), (0, 0), (0, HEAD_PAD - width)))
    return w.reshape(k, N_HEADS * HEAD_PAD)


def _layout_w_in(w):
    d = w.shape[0]
    sp_kr = Q_LORA + KV_LORA
    sp_conv = sp_kr + QK_ROPE
    zeros = lambda n: jnp.zeros((d, n), w.dtype)
    return jnp.concatenate(
        [w[:, :sp_kr], zeros(KPE_OFF), w[:, sp_kr:sp_conv], zeros(LANE - KPE_OFF - QK_ROPE), w[:, sp_conv:]],
        axis=1)


def kernel(x, c, positions, ada_w, ada_b, norm1_g, norm2_g, w_in, q_norm_g, w_uq, kv_norm_g, w_ukv,
           w_o_attn, conv_w, conv_b, conv_ln_g, conv_ln_b, w_conv_out, w_out, router_group_w,
           router_group_b, router_expert_w, router_expert_b, expert_w_gate, expert_w_up,
           expert_w_down, final_norm_g):
    B, S, D = x.shape
    L = ada_w.shape[0]
    T = B * S

    c_pad = jnp.pad(c, ((0, 8 - B), (0, 0)))
    ada = _ada_call(c_pad, ada_w, ada_b)[:, :B].reshape(L, B, 6, D)

    inv_freq = ROPE_BASE ** (-jnp.arange(0, QK_ROPE, 2, dtype=F32) / QK_ROPE)
    freq_row = jnp.zeros((1, LANE), F32)
    freq_row = freq_row.at[0, KPE_OFF:KPE_OFF + QK_ROPE // 2].set(inv_freq)
    freq_row = freq_row.at[0, KPE_OFF + QK_ROPE // 2:KPE_OFF + QK_ROPE].set(inv_freq)
    tc, ts1, ts2 = _rope_call(positions.reshape(T, 1), freq_row)

    x2d = x.reshape(T, D)
    for l in range(L):
        w_in_p = _layout_w_in(w_in[l]).astype(BF16)
        w_uq_p = _pad_heads(w_uq[l], QK_DIM).astype(BF16)
        wkv = w_ukv[l].reshape(KV_LORA, N_HEADS, QK_NOPE + V_DIM)
        w_uk_p = _pad_heads(wkv[:, :, :QK_NOPE].reshape(KV_LORA, -1), QK_NOPE).astype(BF16)
        w_uvt = jnp.pad(wkv[:, :, QK_NOPE:], ((0, 0), (0, 0), (0, VT_ROWS - V_DIM)))
        w_uvt_p = w_uvt.reshape(KV_LORA, N_HEADS * VT_ROWS).T.astype(BF16)
        cw = jnp.pad(conv_w[l], ((0, HALO - CONV_K), (0, 0)))
        q, k, vt, uc, gates = _inproj_call(
            x2d, ada[l], norm1_g[l].reshape(1, D), w_in_p, q_norm_g[l].reshape(1, -1), w_uq_p,
            kv_norm_g[l].reshape(1, -1), w_uk_p, w_uvt_p, tc, ts1, ts2,
            cw, conv_b[l].reshape(1, -1), conv_ln_g[l].reshape(1, -1), conv_ln_b[l].reshape(1, -1), B, S)

        attn = _attn_call(q, k, vt).reshape(T, N_HEADS * V_DIM)

        w_r = jnp.concatenate([router_group_w[l], router_expert_w[l]], axis=1)
        w_r = jnp.pad(w_r, ((0, 0), (0, LANE - w_r.shape[1]))).astype(BF16)
        b_r = jnp.concatenate([router_group_b[l], router_expert_b[l]])
        b_r = jnp.pad(b_r, (0, LANE - b_r.shape[0])).reshape(1, LANE)
        x1, h2, comb, rrow, cnt = _mix_call(
            attn, uc, gates, x2d, ada[l], w_o_attn[l].astype(BF16), w_conv_out[l].astype(BF16),
            w_out[l].astype(BF16), norm2_g[l].reshape(1, D), w_r, b_r, S)

        EF = E_PER_G * D_EXPERT
        wg = expert_w_gate[l].transpose(0, 2, 1, 3).reshape(N_GROUPS, D, EF).astype(BF16)
        wu = expert_w_up[l].transpose(0, 2, 1, 3).reshape(N_GROUPS, D, EF).astype(BF16)
        wd = expert_w_down[l].reshape(N_GROUPS, EF, D).astype(BF16)
        cnt_i = cnt[:, 0, :N_GROUPS].astype(jnp.int32)
        x2d = _moe_call(cnt_i, h2, comb, rrow, x1, ada[l], wg, wu, wd, final_norm_g.reshape(1, D), S,
                        final_norm=(l == L - 1))
    return x2d.reshape(B, S, D)
```

```python
import functools
import math

import jax
import jax.numpy as jnp
from jax import lax
from jax.experimental import pallas as pl
from jax.experimental.pallas import tpu as pltpu

D_MODEL = 1024
N_HEADS = 8
QK_NOPE = 64
QK_ROPE = 32
V_DIM = 64
QK_DIM = QK_NOPE + QK_ROPE
Q_LORA = 384
KV_LORA = 256
ROPE_BASE = 10000.0
CONV_C = 512
CONV_K = 31
N_GROUPS = 4
E_PER_G = 8
D_EXPERT = 128
EPS = 1e-6

LANE = 128
SUBLANE = 8
CONV_TCHUNK = 128
HEAD_PAD = LANE
KPE_OFF = QK_NOPE
VT_ROWS = 80
HALO = 32
VMEM_LIMIT = 56 * 1024 * 1024

C_QLAT = 0
C_KVLAT = Q_LORA
C_KPE = Q_LORA + KV_LORA
C_GLU_A = C_KPE + LANE
C_GLU_B = C_GLU_A + CONV_C
C_GATE = C_GLU_B + CONV_C
IN_COLS_PAD = C_GATE + 2 * D_MODEL

TM_IN = 512
TM_MIX = 512
MOE_CHUNK = 192
TQ = 512
TK = 256

BF16 = jnp.bfloat16
F32 = jnp.float32


def _cparams(sem, flags=None):
    return pltpu.CompilerParams(dimension_semantics=sem, vmem_limit_bytes=VMEM_LIMIT, flags=flags)


def _ada_kernel(c_ref, w_ref, b_ref, o_ref):
    c = c_ref[...]
    c_act = c * jax.nn.sigmoid(c)
    o_ref[0] = jnp.dot(c_act, w_ref[0], preferred_element_type=F32,
                       precision=lax.Precision.HIGHEST) + b_ref[0]


def _ada_call(c_pad, ada_w, ada_b):
    L, D, N = ada_w.shape
    tn = 1536
    return pl.pallas_call(
        _ada_kernel,
        grid=(L, N // tn),
        in_specs=[
            pl.BlockSpec((8, D), lambda l, j: (0, 0)),
            pl.BlockSpec((1, D, tn), lambda l, j: (l, 0, j)),
            pl.BlockSpec((1, 1, tn), lambda l, j: (l, 0, j)),
        ],
        out_specs=pl.BlockSpec((1, 8, tn), lambda l, j: (l, 0, j)),
        out_shape=jax.ShapeDtypeStruct((L, 8, N), F32),
        compiler_params=_cparams(("arbitrary", "arbitrary")),
        name="ada_ln",
    )(c_pad, ada_w, ada_b.reshape(L, 1, N))


def _rope_kernel(pos_ref, freq_ref, c_ref, s1_ref, s2_ref):
    pos = pos_ref[...].astype(F32)
    ang = pos * freq_ref[...]
    lane = lax.broadcasted_iota(jnp.int32, ang.shape, 1)
    cosv = jnp.cos(ang)
    sinv = jnp.sin(ang)
    is_x1 = (lane >= KPE_OFF) & (lane < KPE_OFF + QK_ROPE // 2)
    is_x2 = (lane >= KPE_OFF + QK_ROPE // 2) & (lane < KPE_OFF + QK_ROPE)
    c_ref[...] = jnp.where(is_x1 | is_x2, cosv, jnp.where(lane < KPE_OFF, 1.0, 0.0))
    s1_ref[...] = jnp.where(is_x2, sinv, 0.0)
    s2_ref[...] = jnp.where(is_x1, -sinv, 0.0)


def _rope_call(pos_col, freq_row):
    T = pos_col.shape[0]
    tm = 2048
    spec = pl.BlockSpec((tm, LANE), lambda i: (i, 0))
    return pl.pallas_call(
        _rope_kernel,
        grid=(T // tm,),
        in_specs=[pl.BlockSpec((tm, 1), lambda i: (i, 0)),
                  pl.BlockSpec((1, LANE), lambda i: (0, 0))],
        out_specs=[spec, spec, spec],
        out_shape=[jax.ShapeDtypeStruct((T, LANE), F32)] * 3,
        compiler_params=_cparams(("arbitrary",)),
        name="rope_tables",
    )(pos_col, freq_row)


def _rms(x, g):
    ms = jnp.mean(x * x, axis=-1, keepdims=True)
    return x * lax.rsqrt(ms + EPS) * g


def _rope_apply(x, c, s1, s2):
    half = QK_ROPE // 2
    return x * c + pltpu.roll(x, half, axis=1) * s1 + pltpu.roll(x, LANE - half, axis=1) * s2


def _conv_ln_silu(buf_ref, cw_ref, cb_ref, lng_ref, lnb_ref, tm):
    base = HALO - (CONV_K - 1)
    wrows = CONV_TCHUNK + HALO
    pieces = []
    for t0 in range(0, tm, CONV_TCHUNK):
        row = []
        for c0 in range(0, CONV_C, LANE):
            cs = slice(c0, c0 + LANE)
            win = buf_ref[t0:t0 + wrows, cs]
            acc = jnp.broadcast_to(cb_ref[:, cs], (CONV_TCHUNK, LANE))
            for r in range(SUBLANE):
                shifted = win if r == 0 else pltpu.roll(win, wrows - r, axis=0)
                for o in range(base, base + CONV_K):
                    if o % SUBLANE == r:
                        acc = acc + shifted[o - r:o - r + CONV_TCHUNK] * cw_ref[o - base:o - base + 1, cs]
            row.append(acc)
        pieces.append(jnp.concatenate(row, axis=1))
    acc = jnp.concatenate(pieces, axis=0)
    mu = jnp.mean(acc, axis=-1, keepdims=True)
    xc = acc - mu
    var = jnp.mean(xc * xc, axis=-1, keepdims=True)
    ln = xc * lax.rsqrt(var + EPS) * lng_ref[...] + lnb_ref[...]
    return ln * jax.nn.sigmoid(ln)


def _inproj_kernel(x_ref, ada_ref, g1_ref, win_ref, qg_ref, wuq_ref, kvg_ref, wuk_ref, wuvt_ref,
                   c_ref, s1_ref, s2_ref, cw_ref, cb_ref, lng_ref, lnb_ref,
                   q_ref, k_ref, vt_ref, uc_ref, gate_ref, buf_ref, *, tiles_per_batch):
    tm = x_ref.shape[0]
    first = (pl.program_id(0) % tiles_per_batch) == 0

    @pl.when(first)
    def _():
        buf_ref[0:HALO, :] = jnp.zeros((HALO, CONV_C), F32)

    @pl.when(jnp.logical_not(first))
    def _():
        buf_ref[0:HALO, :] = buf_ref[tm:tm + HALO, :]

    x = x_ref[...]
    sh1 = ada_ref[0, 0:1, :]
    sc1 = ada_ref[0, 1:2, :]
    h = _rms(x, g1_ref[...]) * (1.0 + sc1) + sh1
    z = jnp.dot(h.astype(BF16), win_ref[...], preferred_element_type=F32)

    qn = _rms(z[:, C_QLAT:C_QLAT + Q_LORA], qg_ref[...]).astype(BF16)
    kvn = _rms(z[:, C_KVLAT:C_KVLAT + KV_LORA], kvg_ref[...]).astype(BF16)
    q = jnp.dot(qn, wuq_ref[...], preferred_element_type=F32)
    kk = jnp.dot(kvn, wuk_ref[...], preferred_element_type=F32)
    vvt = lax.dot_general(wuvt_ref[...], kvn, (((1,), (1,)), ((), ())), preferred_element_type=F32)
    vrow = lax.broadcasted_iota(jnp.int32, (VT_ROWS, 1), 0)
    ones_row = jnp.where(vrow == V_DIM, 1.0, 0.0)

    c = c_ref[...]
    s1 = s1_ref[...]
    s2 = s2_ref[...]
    qs = (QK_DIM ** -0.5) * math.log2(math.e)
    cq, s1q, s2q = c * qs, s1 * qs, s2 * qs
    kpe = _rope_apply(z[:, C_KPE:C_KPE + LANE], c, s1, s2)
    for hh in range(N_HEADS):
        sl = slice(hh * HEAD_PAD, (hh + 1) * HEAD_PAD)
        q_ref[0, hh] = _rope_apply(q[:, sl], cq, s1q, s2q).astype(BF16)
        k_ref[0, hh] = (kk[:, sl] + kpe).astype(BF16)
        vth = (vvt[hh * VT_ROWS:(hh + 1) * VT_ROWS, :] + ones_row).astype(BF16)
        for j in range(vt_ref.shape[2]):
            vt_ref[0, hh, j] = vth[:, j * TK:(j + 1) * TK]

    gate_ref[...] = jax.nn.sigmoid(z[:, C_GATE:C_GATE + 2 * D_MODEL]).astype(BF16)

    buf_ref[HALO:HALO + tm, :] = z[:, C_GLU_A:C_GLU_A + CONV_C] * jax.nn.sigmoid(z[:, C_GLU_B:C_GLU_B + CONV_C])
    uc_ref[...] = _conv_ln_silu(buf_ref, cw_ref, cb_ref, lng_ref, lnb_ref, tm).astype(BF16)


def _inproj_call(x2d, ada_l, g1, w_in_p, qg, w_uq_p, kvg, w_uk_p, w_uvt_p, tc, ts1, ts2,
                 cw, cb, lng, lnb, B, S):
    T, D = x2d.shape
    tm = TM_IN
    tpb = S // tm
    kpt = tm // TK
    const = lambda shape: pl.BlockSpec(shape, lambda i: (0,) * len(shape))
    head_spec = pl.BlockSpec((1, N_HEADS, tm, HEAD_PAD), lambda i: (i // tpb, 0, i % tpb, 0))
    head_shape = jax.ShapeDtypeStruct((B, N_HEADS, S, HEAD_PAD), BF16)
    tab = pl.BlockSpec((tm, LANE), lambda i: (i, 0))
    return pl.pallas_call(
        functools.partial(_inproj_kernel, tiles_per_batch=tpb),
        grid=(T // tm,),
        in_specs=[
            pl.BlockSpec((tm, D), lambda i: (i, 0)),
            pl.BlockSpec((1, 6, D), lambda i: (i // tpb, 0, 0)),
            const((1, D)),
            const((D, IN_COLS_PAD)),
            const((1, Q_LORA)),
            const((Q_LORA, N_HEADS * HEAD_PAD)),
            const((1, KV_LORA)),
            const((KV_LORA, N_HEADS * HEAD_PAD)),
            const((N_HEADS * VT_ROWS, KV_LORA)),
            tab, tab, tab,
            const((HALO, CONV_C)), const((1, CONV_C)), const((1, CONV_C)), const((1, CONV_C)),
        ],
        out_specs=[head_spec, head_spec,
                   pl.BlockSpec((1, N_HEADS, kpt, VT_ROWS, TK), lambda i: (i // tpb, 0, i % tpb, 0, 0)),
                   pl.BlockSpec((tm, CONV_C), lambda i: (i, 0)),
                   pl.BlockSpec((tm, 2 * D_MODEL), lambda i: (i, 0))],
        out_shape=[head_shape, head_shape,
                   jax.ShapeDtypeStruct((B, N_HEADS, S // TK, VT_ROWS, TK), BF16),
                   jax.ShapeDtypeStruct((T, CONV_C), BF16),
                   jax.ShapeDtypeStruct((T, 2 * D_MODEL), BF16)],
        scratch_shapes=[pltpu.VMEM((HALO + tm, CONV_C), F32)],
        compiler_params=_cparams(("arbitrary",)),
        name="in_proj",
    )(x2d, ada_l, g1, w_in_p, qg, w_uq_p, kvg, w_uk_p, w_uvt_p, tc, ts1, ts2, cw, cb, lng, lnb)


HEADS_PER_STEP = 8


def _attn_kernel(q_ref, k_ref, vt_ref, o_ref, *scratch):
    qi = pl.program_id(2)
    heads = range(HEADS_PER_STEP)
    assert TQ == 2 * TK
    hs = HEADS_PER_STEP
    s_ref = [scratch[2 * hh:2 * hh + 2] for hh in heads]
    p_ref = [scratch[2 * hs + 2 * hh:2 * hs + 2 * hh + 2] for hh in heads]
    acc_ref = scratch[4 * hs:5 * hs]

    def qk(kb, slot, qlo=0):
        for hh in heads:
            k = k_ref[0, hh, pl.ds(kb * TK, TK), :]
            s_ref[hh][slot][:, qlo:] = lax.dot_general(k, q_ref[0, hh, qlo:, :], (((1,), (1,)), ((), ())),
                                                       preferred_element_type=F32)

    def pv(kb, slot, alpha, qlo=0):
        for hh in heads:
            acc_ref[hh][:, qlo:] = (acc_ref[hh][:, qlo:] * alpha[hh]
                                    + jnp.dot(vt_ref[0, hh, kb], p_ref[hh][slot][:, qlo:],
                                              preferred_element_type=F32))

    def softmax(slot, m, mask=None, qlo=0):
        m_out, alpha = [], []
        for hh in heads:
            s = s_ref[hh][slot][:, qlo:]
            if mask is not None:
                s = jnp.where(mask[:, qlo:], s, -jnp.inf)
            m_old = m[hh][:, qlo:]
            m_new = jnp.maximum(m_old, jnp.max(s, axis=0, keepdims=True))
            p_ref[hh][slot][:, qlo:] = jnp.exp2(s - m_new).astype(BF16)
            alpha.append(jnp.exp2(m_old - m_new))
            m_out.append(m_new)
        return tuple(m_out), tuple(alpha)

    for hh in heads:
        acc_ref[hh][...] = jnp.zeros((VT_ROWS, TQ), F32)
        p_ref[hh][1][...] = jnp.zeros((TK, TQ), BF16)
    m0 = tuple(jnp.full((1, TQ), -jnp.inf, F32) for _ in heads)
    one = tuple(jnp.ones((1, TQ), F32) for _ in heads)
    qk(0, 0)

    def pair(a, carry):
        m, alpha1 = carry
        qk(a + 1, 1)
        pv(jnp.maximum(a - 1, 0), 1, alpha1)
        m, alpha0 = softmax(0, m)
        qk(a + 2, 0)
        pv(a, 0, alpha0)
        m, alpha1 = softmax(1, m)
        return m, alpha1

    m, alpha1 = lax.fori_loop(0, qi, lambda i, c: pair(2 * i, c), (m0, one))
    a = 2 * qi
    key = lax.broadcasted_iota(jnp.int32, (TK, TQ), 0)
    qry = lax.broadcasted_iota(jnp.int32, (TK, TQ), 1)
    qk(a + 1, 1, qlo=TK)
    pv(jnp.maximum(a - 1, 0), 1, alpha1)
    m, alpha0 = softmax(0, m, key <= qry)
    pv(a, 0, alpha0)
    _, alpha1 = softmax(1, m, key + TK <= qry, qlo=TK)
    pv(a + 1, 1, alpha1, qlo=TK)
    outs = []
    for hh in heads:
        acc = acc_ref[hh][...]
        o_t = acc[:V_DIM, :] / acc[V_DIM:V_DIM + 1, :]
        outs.append(o_t.T)
    o_ref[0] = jnp.concatenate(outs, axis=-1).astype(BF16)


def _attn_call(q, k, vt):
    B, H, S, _ = q.shape
    nq = S // TQ
    hs = HEADS_PER_STEP
    return pl.pallas_call(
        _attn_kernel,
        grid=(B, H // hs, nq),
        in_specs=[
            pl.BlockSpec((1, hs, TQ, HEAD_PAD), lambda b, h, i: (b, h, i, 0)),
            pl.BlockSpec((1, hs, S, HEAD_PAD), lambda b, h, i: (b, h, 0, 0),
                         pipeline_mode=pl.Buffered(1)),
            pl.BlockSpec((1, hs, S // TK, VT_ROWS, TK), lambda b, h, i: (b, h, 0, 0, 0),
                         pipeline_mode=pl.Buffered(1)),
        ],
        out_specs=pl.BlockSpec((1, TQ, hs * V_DIM), lambda b, h, i: (b, i, h)),
        out_shape=jax.ShapeDtypeStruct((B, S, H * V_DIM), BF16),
        scratch_shapes=([pltpu.VMEM((TK, TQ), F32)] * (2 * hs) + [pltpu.VMEM((TK, TQ), BF16)] * (2 * hs)
                        + [pltpu.VMEM((VT_ROWS, TQ), F32)] * hs),
        compiler_params=_cparams(("arbitrary", "arbitrary", "arbitrary")),
        name="mla_attention",
    )(q, k, vt)


def _mix_kernel(attn_ref, uc_ref, gate_ref, x_ref, ada_ref, woa_ref, wco_ref, wout_ref, g2_ref,
                wr_ref, br_ref, triu_ref, x1_ref, h2_ref, comb_ref, rrow_ref, cnt_ref):
    y_conv = jnp.dot(uc_ref[...], wco_ref[...], preferred_element_type=F32)
    y_attn = jnp.dot(attn_ref[...], woa_ref[...], preferred_element_type=F32)

    g_attn = gate_ref[:, 0:D_MODEL].astype(F32)
    g_conv = gate_ref[:, D_MODEL:2 * D_MODEL].astype(F32)
    merged = g_attn * y_attn + g_conv * y_conv
    y = jnp.dot(merged.astype(BF16), wout_ref[...], preferred_element_type=F32)
    gt1 = ada_ref[0, 2:3, :]
    x1 = x_ref[...] + gt1 * y
    x1_ref[...] = x1

    sh2 = ada_ref[0, 3:4, :]
    sc2 = ada_ref[0, 4:5, :]
    h2 = (_rms(x1, g2_ref[...]) * (1.0 + sc2) + sh2).astype(BF16)
    h2_ref[...] = h2

    logits = jnp.dot(h2, wr_ref[...], preferred_element_type=F32) + br_ref[...]
    lane = lax.broadcasted_iota(jnp.int32, logits.shape, 1)
    big = jnp.int32(1 << 20)
    neg = -jnp.inf
    lg = jnp.where(lane < N_GROUPS, logits, neg)
    gmax = jnp.max(lg, axis=-1, keepdims=True)
    g_val = 1.0 / jnp.sum(jnp.exp(lg - gmax), axis=-1, keepdims=True)
    g_idx = jnp.min(jnp.where(lg == gmax, lane, big), axis=-1, keepdims=True)
    e_lo = N_GROUPS + g_idx * E_PER_G
    emask = (lane >= e_lo) & (lane < e_lo + E_PER_G)
    le = jnp.where(emask, logits, neg)
    m1 = jnp.max(le, axis=-1, keepdims=True)
    i1 = jnp.min(jnp.where(le == m1, lane, big), axis=-1, keepdims=True)
    le2 = jnp.where(lane == i1, neg, le)
    m2 = jnp.max(le2, axis=-1, keepdims=True)
    i2 = jnp.min(jnp.where(le2 == m2, lane, big), axis=-1, keepdims=True)
    z_e = jnp.sum(jnp.exp(le - m1), axis=-1, keepdims=True)
    p1 = 1.0 / z_e
    p2 = jnp.exp(m2 - m1) / z_e
    w1 = g_val * (p1 / (p1 + p2))
    w2 = g_val * (p2 / (p1 + p2))
    comb = jnp.where(lane == i1, w1, 0.0) + jnp.where(lane == i2, w2, 0.0)
    onehot = jnp.where(lane == g_idx, 1.0, 0.0)
    onehot_t = onehot.T[0:2 * SUBLANE, :]
    rank_row = jnp.dot(onehot_t.astype(BF16), triu_ref[...], preferred_element_type=F32)
    rrow_ref[0] = jnp.where(onehot_t > 0.0, rank_row, -1.0)[0:SUBLANE, :]
    cnt_ref[0] = jnp.broadcast_to(jnp.sum(onehot, axis=0, keepdims=True), (SUBLANE, LANE))
    for g in range(N_GROUPS):
        shifted = pltpu.roll(comb, LANE - (N_GROUPS + g * E_PER_G), axis=1)
        comb_ref[:, g * LANE:(g + 1) * LANE] = jnp.where(lane < E_PER_G, shifted, 0.0)


def _mix_call(attn2d, uc2d, gates, x2d, ada_l, woa, wco, wout, g2, wr, br, S):
    T, D = x2d.shape
    tm = TM_MIX
    tpb = S // tm
    triu = jnp.tri(tm, tm, -1, dtype=BF16).T
    const = lambda shape: pl.BlockSpec(shape, lambda i: (0,) * len(shape))
    row = lambda w: pl.BlockSpec((tm, w), lambda i: (i, 0))
    return pl.pallas_call(
        _mix_kernel,
        grid=(T // tm,),
        in_specs=[
            row(N_HEADS * V_DIM),
            row(CONV_C),
            row(2 * D),
            row(D),
            pl.BlockSpec((1, 6, D), lambda i: (i // tpb, 0, 0)),
            const((N_HEADS * V_DIM, D)),
            const((CONV_C, D)),
            const((D, D)),
            const((1, D)),
            const((D, LANE)),
            const((1, LANE)),
            const((tm, tm)),
        ],
        out_specs=[row(D), row(D), row(N_GROUPS * LANE),
                   pl.BlockSpec((1, SUBLANE, tm), lambda i: (i, 0, 0)),
                   pl.BlockSpec((1, SUBLANE, LANE), lambda i: (i, 0, 0))],
        out_shape=[jax.ShapeDtypeStruct((T, D), F32),
                   jax.ShapeDtypeStruct((T, D), BF16),
                   jax.ShapeDtypeStruct((T, N_GROUPS * LANE), F32),
                   jax.ShapeDtypeStruct((T // tm, SUBLANE, tm), F32),
                   jax.ShapeDtypeStruct((T // tm, SUBLANE, LANE), F32)],
        compiler_params=_cparams(("arbitrary",)),
        name="mixer_merge",
    )(attn2d, uc2d, gates, x2d, ada_l, woa, wco, wout, g2, wr, br, triu)


def _moe_kernel(cnt_ref, h_ref, comb_ref, rrow_ref, x1_ref, ada_ref, wg_ref, wu_ref, wd_ref, gf_ref,
                o_ref, acc_ref, *, final_norm):
    i = pl.program_id(0)
    tm = h_ref.shape[0]
    acc_ref[...] = jnp.zeros_like(acc_ref)
    row_id = lax.broadcasted_iota(jnp.int32, (MOE_CHUNK, tm), 0).astype(F32)

    for g in range(N_GROUPS):
        rank_row = rrow_ref[0, g:g + 1, :]
        cw = comb_ref[:, g * LANE:(g + 1) * LANE]
        cw_hi = cw.astype(BF16)
        cw_lo = (cw - cw_hi.astype(F32)).astype(BF16)
        cw2 = jnp.concatenate([cw_hi, cw_lo], axis=1)

        def chunk(c, carry, g=g, rank_row=rank_row, cw2=cw2):
            r0 = (c * MOE_CHUNK).astype(F32)
            sel = jnp.where(row_id + r0 == rank_row, 1.0, 0.0).astype(BF16)
            hc = jnp.dot(sel, h_ref[...], preferred_element_type=F32).astype(BF16)
            cwc = jnp.dot(sel, cw2, preferred_element_type=F32)
            cwc = cwc[:, :LANE] + cwc[:, LANE:]
            a = jnp.dot(hc, wg_ref[g], preferred_element_type=F32)
            u = jnp.dot(hc, wu_ref[g], preferred_element_type=F32)
            parts = []
            for e in range(E_PER_G):
                sl = slice(e * D_EXPERT, (e + 1) * D_EXPERT)
                ae = a[:, sl]
                ce = jnp.broadcast_to(cwc[:, e:e + 1], (MOE_CHUNK, D_EXPERT))
                parts.append((ae * jax.nn.sigmoid(ae) * u[:, sl] * ce).astype(BF16))
            y = jnp.dot(jnp.concatenate(parts, axis=-1), wd_ref[g], preferred_element_type=F32)
            acc_ref[...] += lax.dot_general(sel, y.astype(BF16), (((0,), (0,)), ((), ())),
                                            preferred_element_type=F32)
            return carry

        n = cnt_ref[i, g]
        n_chunks = sum((n > k * MOE_CHUNK).astype(jnp.int32) for k in range(pl.cdiv(tm, MOE_CHUNK)))
        lax.fori_loop(0, n_chunks, chunk, 0)

    gt2 = ada_ref[0, 5:6, :]
    x2 = x1_ref[...] + gt2 * acc_ref[...]
    if final_norm:
        x2 = _rms(x2, gf_ref[...])
    o_ref[...] = x2


def _moe_call(cnt, h2, comb, rrow, x1, ada_l, wg, wu, wd, gf, S, final_norm):
    T, D = x1.shape
    tm = TM_MIX
    tpb = S // tm
    EF = E_PER_G * D_EXPERT
    resident = lambda shape: pl.BlockSpec(shape, lambda i, c: (0,) * len(shape),
                                          pipeline_mode=pl.Buffered(1))
    return pl.pallas_call(
        functools.partial(_moe_kernel, final_norm=final_norm),
        grid_spec=pltpu.PrefetchScalarGridSpec(
            num_scalar_prefetch=1,
            grid=(T // tm,),
            in_specs=[
                pl.BlockSpec((tm, D), lambda i, c: (i, 0)),
                pl.BlockSpec((tm, N_GROUPS * LANE), lambda i, c: (i, 0)),
                pl.BlockSpec((1, SUBLANE, tm), lambda i, c: (i, 0, 0)),
                pl.BlockSpec((tm, D), lambda i, c: (i, 0)),
                pl.BlockSpec((1, 6, D), lambda i, c: (i // tpb, 0, 0)),
                resident((N_GROUPS, D, EF)),
                resident((N_GROUPS, D, EF)),
                resident((N_GROUPS, EF, D)),
                pl.BlockSpec((1, D), lambda i, c: (0, 0)),
            ],
            out_specs=pl.BlockSpec((tm, D), lambda i, c: (i, 0)),
            scratch_shapes=[pltpu.VMEM((tm, D), F32)],
        ),
        out_shape=jax.ShapeDtypeStruct((T, D), F32),
        compiler_params=_cparams(("arbitrary",)),
        name="moe_experts",
    )(cnt, h2, comb, rrow, x1, ada_l, wg, wu, wd, gf)


def _pad_heads(w, width):
    k = w.shape[0]
    w = w.reshape(k, N_HEADS, width)
    w = jnp.pad(w, ((0, 0), (0, 0), (0, HEAD_PAD - width)))
    return w.reshape(k, N_HEADS * HEAD_PAD)


def _layout_w_in(w):
    d = w.shape[0]
    sp_kr = Q_LORA + KV_LORA
    sp_conv = sp_kr + QK_ROPE
    zeros = lambda n: jnp.zeros((d, n), w.dtype)
    return jnp.concatenate(
        [w[:, :sp_kr], zeros(KPE_OFF), w[:, sp_kr:sp_conv], zeros(LANE - KPE_OFF - QK_ROPE), w[:, sp_conv:]],
        axis=1)


def kernel(x, c, positions, ada_w, ada_b, norm1_g, norm2_g, w_in, q_norm_g, w_uq, kv_norm_g, w_ukv,
           w_o_attn, conv_w, conv_b, conv_ln_g, conv_ln_b, w_conv_out, w_out, router_group_w,
           router_group_b, router_expert_w, router_expert_b, expert_w_gate, expert_w_up,
           expert_w_down, final_norm_g):
    B, S, D = x.shape
    L = ada_w.shape[0]
    T = B * S

    c_pad = jnp.pad(c, ((0, 8 - B), (0, 0)))
    ada = _ada_call(c_pad, ada_w, ada_b)[:, :B].reshape(L, B, 6, D)

    inv_freq = ROPE_BASE ** (-jnp.arange(0, QK_ROPE, 2, dtype=F32) / QK_ROPE)
    freq_row = jnp.zeros((1, LANE), F32)
    freq_row = freq_row.at[0, KPE_OFF:KPE_OFF + QK_ROPE // 2].set(inv_freq)
    freq_row = freq_row.at[0, KPE_OFF + QK_ROPE // 2:KPE_OFF + QK_ROPE].set(inv_freq)
    tc, ts1, ts2 = _rope_call(positions.reshape(T, 1), freq_row)

    x2d = x.reshape(T, D)
    for l in range(L):
        w_in_p = _layout_w_in(w_in[l]).astype(BF16)
        w_uq_p = _pad_heads(w_uq[l], QK_DIM).astype(BF16)
        wkv = w_ukv[l].reshape(KV_LORA, N_HEADS, QK_NOPE + V_DIM)
        w_uk_p = _pad_heads(wkv[:, :, :QK_NOPE].reshape(KV_LORA, -1), QK_NOPE).astype(BF16)
        w_uvt = jnp.pad(wkv[:, :, QK_NOPE:], ((0, 0), (0, 0), (0, VT_ROWS - V_DIM)))
        w_uvt_p = w_uvt.reshape(KV_LORA, N_HEADS * VT_ROWS).T.astype(BF16)
        cw = jnp.pad(conv_w[l], ((0, HALO - CONV_K), (0, 0)))
        q, k, vt, uc, gates = _inproj_call(
            x2d, ada[l], norm1_g[l].reshape(1, D), w_in_p, q_norm_g[l].reshape(1, -1), w_uq_p,
            kv_norm_g[l].reshape(1, -1), w_uk_p, w_uvt_p, tc, ts1, ts2,
            cw, conv_b[l].reshape(1, -1), conv_ln_g[l].reshape(1, -1), conv_ln_b[l].reshape(1, -1), B, S)

        attn = _attn_call(q, k, vt).reshape(T, N_HEADS * V_DIM)

        w_r = jnp.concatenate([router_group_w[l], router_expert_w[l]], axis=1)
        w_r = jnp.pad(w_r, ((0, 0), (0, LANE - w_r.shape[1]))).astype(BF16)
        b_r = jnp.concatenate([router_group_b[l], router_expert_b[l]])
        b_r = jnp.pad(b_r, (0, LANE - b_r.shape[0])).reshape(1, LANE)
        x1, h2, comb, rrow, cnt = _mix_call(
            attn, uc, gates, x2d, ada[l], w_o_attn[l].astype(BF16), w_conv_out[l].astype(BF16),
            w_out[l].astype(BF16), norm2_g[l].reshape(1, D), w_r, b_r, S)

        EF = E_PER_G * D_EXPERT
        wg = expert_w_gate[l].transpose(0, 2, 1, 3).reshape(N_GROUPS, D, EF).astype(BF16)
        wu = expert_w_up[l].transpose(0, 2, 1, 3).reshape(N_GROUPS, D, EF).astype(BF16)
        wd = expert_w_down[l].reshape(N_GROUPS, EF, D).astype(BF16)
        cnt_i = cnt[:, 0, :N_GROUPS].astype(jnp.int32)
        x2d = _moe_call(cnt_i, h2, comb, rrow, x1, ada[l], wg, wu, wd, final_norm_g.reshape(1, D), S,
                        final_norm=(l == L - 1))
    return x2d.reshape(B, S, D)
```

```python
import functools
import math

import jax
import jax.numpy as jnp
from jax import lax
from jax.experimental import pallas as pl
from jax.experimental.pallas import tpu as pltpu

D_MODEL = 1024
N_HEADS = 8
QK_NOPE = 64
QK_ROPE = 32
V_DIM = 64
QK_DIM = QK_NOPE + QK_ROPE
Q_LORA = 384
KV_LORA = 256
ROPE_BASE = 10000.0
CONV_C = 512
CONV_K = 31
N_GROUPS = 4
E_PER_G = 8
D_EXPERT = 128
EPS = 1e-6

LANE = 128
SUBLANE = 8
CONV_TCHUNK = 128
HEAD_PAD = LANE
KPE_OFF = QK_NOPE
VT_ROWS = 80
ROUTER_ROWS = 48
HALO = 32
VMEM_LIMIT = 56 * 1024 * 1024

C_QLAT = 0
C_KVLAT = Q_LORA
C_KPE = Q_LORA + KV_LORA
C_GLU_A = C_KPE + LANE
C_GLU_B = C_GLU_A + CONV_C
C_GATE = C_GLU_B + CONV_C
IN_COLS_PAD = C_GATE + 2 * D_MODEL

TM_IN = 512
TM_MIX = 512
MOE_CHUNK = 192
TQ = 512
TK = 256

BF16 = jnp.bfloat16
F32 = jnp.float32


def _cparams(sem, flags=None):
    return pltpu.CompilerParams(dimension_semantics=sem, vmem_limit_bytes=VMEM_LIMIT, flags=flags)


def _ada_kernel(c_ref, w_ref, b_ref, o_ref):
    c = c_ref[...]
    c_act = c * jax.nn.sigmoid(c)
    o_ref[0] = jnp.dot(c_act, w_ref[0], preferred_element_type=F32,
                       precision=lax.Precision.HIGHEST) + b_ref[0]


def _ada_call(c_pad, ada_w, ada_b):
    L, D, N = ada_w.shape
    tn = 1536
    return pl.pallas_call(
        _ada_kernel,
        grid=(L, N // tn),
        in_specs=[
            pl.BlockSpec((8, D), lambda l, j: (0, 0)),
            pl.BlockSpec((1, D, tn), lambda l, j: (l, 0, j)),
            pl.BlockSpec((1, 1, tn), lambda l, j: (l, 0, j)),
        ],
        out_specs=pl.BlockSpec((1, 8, tn), lambda l, j: (l, 0, j)),
        out_shape=jax.ShapeDtypeStruct((L, 8, N), F32),
        compiler_params=_cparams(("arbitrary", "arbitrary")),
        name="ada_ln",
    )(c_pad, ada_w, ada_b.reshape(L, 1, N))


def _rope_kernel(pos_ref, freq_ref, c_ref, s1_ref, s2_ref):
    pos = pos_ref[...].astype(F32)
    ang = pos * freq_ref[...]
    lane = lax.broadcasted_iota(jnp.int32, ang.shape, 1)
    cosv = jnp.cos(ang)
    sinv = jnp.sin(ang)
    is_x1 = (lane >= KPE_OFF) & (lane < KPE_OFF + QK_ROPE // 2)
    is_x2 = (lane >= KPE_OFF + QK_ROPE // 2) & (lane < KPE_OFF + QK_ROPE)
    c_ref[...] = jnp.where(is_x1 | is_x2, cosv, jnp.where(lane < KPE_OFF, 1.0, 0.0))
    s1_ref[...] = jnp.where(is_x2, sinv, 0.0)
    s2_ref[...] = jnp.where(is_x1, -sinv, 0.0)


def _rope_call(pos_col, freq_row):
    T = pos_col.shape[0]
    tm = 2048
    spec = pl.BlockSpec((tm, LANE), lambda i: (i, 0))
    return pl.pallas_call(
        _rope_kernel,
        grid=(T // tm,),
        in_specs=[pl.BlockSpec((tm, 1), lambda i: (i, 0)),
                  pl.BlockSpec((1, LANE), lambda i: (0, 0))],
        out_specs=[spec, spec, spec],
        out_shape=[jax.ShapeDtypeStruct((T, LANE), F32)] * 3,
        compiler_params=_cparams(("arbitrary",)),
        name="rope_tables",
    )(pos_col, freq_row)


def _rms(x, g):
    ms = jnp.mean(x * x, axis=-1, keepdims=True)
    return x * lax.rsqrt(ms + EPS) * g


def _rope_apply(x, c, s1, s2):
    half = QK_ROPE // 2
    return x * c + pltpu.roll(x, half, axis=1) * s1 + pltpu.roll(x, LANE - half, axis=1) * s2


def _conv_ln_silu(buf_ref, cw_ref, cb_ref, lng_ref, lnb_ref, tm):
    base = HALO - (CONV_K - 1)
    wrows = CONV_TCHUNK + HALO
    pieces = []
    for t0 in range(0, tm, CONV_TCHUNK):
        row = []
        for c0 in range(0, CONV_C, LANE):
            cs = slice(c0, c0 + LANE)
            win = buf_ref[t0:t0 + wrows, cs]
            acc = jnp.broadcast_to(cb_ref[:, cs], (CONV_TCHUNK, LANE))
            for r in range(SUBLANE):
                shifted = win if r == 0 else pltpu.roll(win, wrows - r, axis=0)
                for o in range(base, base + CONV_K):
                    if o % SUBLANE == r:
                        acc = acc + shifted[o - r:o - r + CONV_TCHUNK] * cw_ref[o - base:o - base + 1, cs]
            row.append(acc)
        pieces.append(jnp.concatenate(row, axis=1))
    acc = jnp.concatenate(pieces, axis=0)
    mu = jnp.mean(acc, axis=-1, keepdims=True)
    xc = acc - mu
    var = jnp.mean(xc * xc, axis=-1, keepdims=True)
    ln = xc * lax.rsqrt(var + EPS) * lng_ref[...] + lnb_ref[...]
    return ln * jax.nn.sigmoid(ln)


def _inproj_kernel(x_ref, ada_ref, g1_ref, win_ref, qg_ref, wuq_ref, kvg_ref, wuk_ref, wuvt_ref,
                   c_ref, s1_ref, s2_ref, cw_ref, cb_ref, lng_ref, lnb_ref,
                   q_ref, k_ref, vt_ref, uc_ref, gate_ref, buf_ref, *, tiles_per_batch):
    tm = x_ref.shape[0]
    first = (pl.program_id(0) % tiles_per_batch) == 0

    @pl.when(first)
    def _():
        buf_ref[0:HALO, :] = jnp.zeros((HALO, CONV_C), F32)

    @pl.when(jnp.logical_not(first))
    def _():
        buf_ref[0:HALO, :] = buf_ref[tm:tm + HALO, :]

    x = x_ref[...]
    sh1 = ada_ref[0, 0:1, :]
    sc1 = ada_ref[0, 1:2, :]
    h = _rms(x, g1_ref[...]) * (1.0 + sc1) + sh1
    z = jnp.dot(h.astype(BF16), win_ref[...], preferred_element_type=F32)

    qn = _rms(z[:, C_QLAT:C_QLAT + Q_LORA], qg_ref[...]).astype(BF16)
    kvn = _rms(z[:, C_KVLAT:C_KVLAT + KV_LORA], kvg_ref[...]).astype(BF16)
    q = jnp.dot(qn, wuq_ref[...], preferred_element_type=F32)
    kk = jnp.dot(kvn, wuk_ref[...], preferred_element_type=F32)
    vvt = lax.dot_general(wuvt_ref[...], kvn, (((1,), (1,)), ((), ())), preferred_element_type=F32)
    vrow = lax.broadcasted_iota(jnp.int32, (VT_ROWS, 1), 0)
    ones_row = jnp.where(vrow == V_DIM, 1.0, 0.0)

    c = c_ref[...]
    s1 = s1_ref[...]
    s2 = s2_ref[...]
    qs = (QK_DIM ** -0.5) * math.log2(math.e)
    cq, s1q, s2q = c * qs, s1 * qs, s2 * qs
    kpe = _rope_apply(z[:, C_KPE:C_KPE + LANE], c, s1, s2)
    for hh in range(N_HEADS):
        sl = slice(hh * HEAD_PAD, (hh + 1) * HEAD_PAD)
        q_ref[0, hh] = _rope_apply(q[:, sl], cq, s1q, s2q).astype(BF16)
        k_ref[0, hh] = (kk[:, sl] + kpe).astype(BF16)
        vth = (vvt[hh * VT_ROWS:(hh + 1) * VT_ROWS, :] + ones_row).astype(BF16)
        for j in range(vt_ref.shape[2]):
            vt_ref[0, hh, j] = vth[:, j * TK:(j + 1) * TK]

    gate_ref[...] = jax.nn.sigmoid(z[:, C_GATE:C_GATE + 2 * D_MODEL]).astype(BF16)

    buf_ref[HALO:HALO + tm, :] = z[:, C_GLU_A:C_GLU_A + CONV_C] * jax.nn.sigmoid(z[:, C_GLU_B:C_GLU_B + CONV_C])
    uc_ref[...] = _conv_ln_silu(buf_ref, cw_ref, cb_ref, lng_ref, lnb_ref, tm).astype(BF16)


def _inproj_call(x2d, ada_l, g1, w_in_p, qg, w_uq_p, kvg, w_uk_p, w_uvt_p, tc, ts1, ts2,
                 cw, cb, lng, lnb, B, S):
    T, D = x2d.shape
    tm = TM_IN
    tpb = S // tm
    kpt = tm // TK
    const = lambda shape: pl.BlockSpec(shape, lambda i: (0,) * len(shape))
    head_spec = pl.BlockSpec((1, N_HEADS, tm, HEAD_PAD), lambda i: (i // tpb, 0, i % tpb, 0))
    head_shape = jax.ShapeDtypeStruct((B, N_HEADS, S, HEAD_PAD), BF16)
    tab = pl.BlockSpec((tm, LANE), lambda i: (i, 0))
    return pl.pallas_call(
        functools.partial(_inproj_kernel, tiles_per_batch=tpb),
        grid=(T // tm,),
        in_specs=[
            pl.BlockSpec((tm, D), lambda i: (i, 0)),
            pl.BlockSpec((1, 6, D), lambda i: (i // tpb, 0, 0)),
            const((1, D)),
            const((D, IN_COLS_PAD)),
            const((1, Q_LORA)),
            const((Q_LORA, N_HEADS * HEAD_PAD)),
            const((1, KV_LORA)),
            const((KV_LORA, N_HEADS * HEAD_PAD)),
            const((N_HEADS * VT_ROWS, KV_LORA)),
            tab, tab, tab,
            const((HALO, CONV_C)), const((1, CONV_C)), const((1, CONV_C)), const((1, CONV_C)),
        ],
        out_specs=[head_spec, head_spec,
                   pl.BlockSpec((1, N_HEADS, kpt, VT_ROWS, TK), lambda i: (i // tpb, 0, i % tpb, 0, 0)),
                   pl.BlockSpec((tm, CONV_C), lambda i: (i, 0)),
                   pl.BlockSpec((tm, 2 * D_MODEL), lambda i: (i, 0))],
        out_shape=[head_shape, head_shape,
                   jax.ShapeDtypeStruct((B, N_HEADS, S // TK, VT_ROWS, TK), BF16),
                   jax.ShapeDtypeStruct((T, CONV_C), BF16),
                   jax.ShapeDtypeStruct((T, 2 * D_MODEL), BF16)],
        scratch_shapes=[pltpu.VMEM((HALO + tm, CONV_C), F32)],
        compiler_params=_cparams(("arbitrary",)),
        name="in_proj",
    )(x2d, ada_l, g1, w_in_p, qg, w_uq_p, kvg, w_uk_p, w_uvt_p, tc, ts1, ts2, cw, cb, lng, lnb)


HEADS_PER_STEP = 8


def _attn_kernel(q_ref, k_ref, vt_ref, o_ref, *scratch):
    qi = pl.program_id(2)
    heads = range(HEADS_PER_STEP)
    assert TQ == 2 * TK
    hs = HEADS_PER_STEP
    s_ref = [scratch[2 * hh:2 * hh + 2] for hh in heads]
    p_ref = [scratch[2 * hs + 2 * hh:2 * hs + 2 * hh + 2] for hh in heads]
    acc_ref = scratch[4 * hs:5 * hs]

    def qk(kb, slot, qlo=0):
        for hh in heads:
            k = k_ref[0, hh, pl.ds(kb * TK, TK), :]
            s_ref[hh][slot][:, qlo:] = lax.dot_general(k, q_ref[0, hh, qlo:, :], (((1,), (1,)), ((), ())),
                                                       preferred_element_type=F32)

    def pv(kb, slot, alpha, qlo=0):
        for hh in heads:
            acc_ref[hh][:, qlo:] = (acc_ref[hh][:, qlo:] * alpha[hh]
                                    + jnp.dot(vt_ref[0, hh, kb], p_ref[hh][slot][:, qlo:],
                                              preferred_element_type=F32))

    def softmax(slot, m, mask=None, qlo=0):
        m_out, alpha = [], []
        for hh in heads:
            s = s_ref[hh][slot][:, qlo:]
            if mask is not None:
                s = jnp.where(mask[:, qlo:], s, -jnp.inf)
            m_old = m[hh][:, qlo:]
            m_new = jnp.maximum(m_old, jnp.max(s, axis=0, keepdims=True))
            p_ref[hh][slot][:, qlo:] = jnp.exp2(s - m_new).astype(BF16)
            alpha.append(jnp.exp2(m_old - m_new))
            m_out.append(m_new)
        return tuple(m_out), tuple(alpha)

    for hh in heads:
        acc_ref[hh][...] = jnp.zeros((VT_ROWS, TQ), F32)
        p_ref[hh][1][...] = jnp.zeros((TK, TQ), BF16)
    m0 = tuple(jnp.full((1, TQ), -jnp.inf, F32) for _ in heads)
    one = tuple(jnp.ones((1, TQ), F32) for _ in heads)
    qk(0, 0)

    def pair(a, carry):
        m, alpha1 = carry
        qk(a + 1, 1)
        pv(jnp.maximum(a - 1, 0), 1, alpha1)
        m, alpha0 = softmax(0, m)
        qk(a + 2, 0)
        pv(a, 0, alpha0)
        m, alpha1 = softmax(1, m)
        return m, alpha1

    m, alpha1 = lax.fori_loop(0, qi, lambda i, c: pair(2 * i, c), (m0, one))
    a = 2 * qi
    key = lax.broadcasted_iota(jnp.int32, (TK, TQ), 0)
    qry = lax.broadcasted_iota(jnp.int32, (TK, TQ), 1)
    qk(a + 1, 1, qlo=TK)
    pv(jnp.maximum(a - 1, 0), 1, alpha1)
    m, alpha0 = softmax(0, m, key <= qry)
    pv(a, 0, alpha0)
    _, alpha1 = softmax(1, m, key + TK <= qry, qlo=TK)
    pv(a + 1, 1, alpha1, qlo=TK)
    outs = []
    for hh in heads:
        acc = acc_ref[hh][...]
        o_t = acc[:V_DIM, :] / acc[V_DIM:V_DIM + 1, :]
        outs.append(o_t.T)
    o_ref[0] = jnp.concatenate(outs, axis=-1).astype(BF16)


def _attn_call(q, k, vt):
    B, H, S, _ = q.shape
    nq = S // TQ
    hs = HEADS_PER_STEP
    return pl.pallas_call(
        _attn_kernel,
        grid=(B, H // hs, nq),
        in_specs=[
            pl.BlockSpec((1, hs, TQ, HEAD_PAD), lambda b, h, i: (b, h, i, 0)),
            pl.BlockSpec((1, hs, S, HEAD_PAD), lambda b, h, i: (b, h, 0, 0),
                         pipeline_mode=pl.Buffered(1)),
            pl.BlockSpec((1, hs, S // TK, VT_ROWS, TK), lambda b, h, i: (b, h, 0, 0, 0),
                         pipeline_mode=pl.Buffered(1)),
        ],
        out_specs=pl.BlockSpec((1, TQ, hs * V_DIM), lambda b, h, i: (b, i, h)),
        out_shape=jax.ShapeDtypeStruct((B, S, H * V_DIM), BF16),
        scratch_shapes=([pltpu.VMEM((TK, TQ), F32)] * (2 * hs) + [pltpu.VMEM((TK, TQ), BF16)] * (2 * hs)
                        + [pltpu.VMEM((VT_ROWS, TQ), F32)] * hs),
        compiler_params=_cparams(("arbitrary", "arbitrary", "arbitrary")),
        name="mla_attention",
    )(q, k, vt)


def _mix_kernel(attn_ref, uc_ref, gate_ref, x_ref, ada_ref, woa_ref, wco_ref, wout_ref, g2_ref,
                wr_ref, br_ref, triu_ref, x1_ref, h2_ref, comb_ref, rrow_ref, cnt_ref):
    y_conv = jnp.dot(uc_ref[...], wco_ref[...], preferred_element_type=F32)
    y_attn = jnp.dot(attn_ref[...], woa_ref[...], preferred_element_type=F32)

    g_attn = gate_ref[:, 0:D_MODEL].astype(F32)
    g_conv = gate_ref[:, D_MODEL:2 * D_MODEL].astype(F32)
    merged = g_attn * y_attn + g_conv * y_conv
    y = jnp.dot(merged.astype(BF16), wout_ref[...], preferred_element_type=F32)
    gt1 = ada_ref[0, 2:3, :]
    x1 = x_ref[...] + gt1 * y
    x1_ref[...] = x1

    sh2 = ada_ref[0, 3:4, :]
    sc2 = ada_ref[0, 4:5, :]
    h2 = (_rms(x1, g2_ref[...]) * (1.0 + sc2) + sh2).astype(BF16)
    h2_ref[...] = h2

    tm = h2.shape[0]
    logits_t = lax.dot_general(wr_ref[...], h2, (((1,), (1,)), ((), ())),
                               preferred_element_type=F32) + br_ref[...]
    row = lax.broadcasted_iota(jnp.int32, (SUBLANE, tm), 0)
    big = jnp.int32(1 << 20)
    neg = -jnp.inf
    lg = jnp.where(row < N_GROUPS, logits_t[0:SUBLANE], neg)
    gmax = jnp.max(lg, axis=0, keepdims=True)
    g_val = 1.0 / jnp.sum(jnp.exp(lg - gmax), axis=0, keepdims=True)
    g_idx = jnp.min(jnp.where(lg == gmax, row, big), axis=0, keepdims=True)
    le = logits_t[SUBLANE:2 * SUBLANE]
    for g in range(1, N_GROUPS):
        le = jnp.where(g_idx == g, logits_t[(g + 1) * SUBLANE:(g + 2) * SUBLANE], le)
    m1 = jnp.max(le, axis=0, keepdims=True)
    i1 = jnp.min(jnp.where(le == m1, row, big), axis=0, keepdims=True)
    le2 = jnp.where(row == i1, neg, le)
    m2 = jnp.max(le2, axis=0, keepdims=True)
    i2 = jnp.min(jnp.where(le2 == m2, row, big), axis=0, keepdims=True)
    z_e = jnp.sum(jnp.exp(le - m1), axis=0, keepdims=True)
    p1 = 1.0 / z_e
    p2 = jnp.exp(m2 - m1) / z_e
    w1 = g_val * (p1 / (p1 + p2))
    w2 = g_val * (p2 / (p1 + p2))
    slab = jnp.where(row == i1, w1, 0.0) + jnp.where(row == i2, w2, 0.0)
    comb_t = jnp.concatenate([jnp.where(g_idx == g, slab, 0.0) for g in range(N_GROUPS)]
                             + [jnp.zeros((LANE - N_GROUPS * E_PER_G, tm), F32)], axis=0)
    comb = comb_t.T
    onehot_t = jnp.where(row == g_idx, 1.0, 0.0)
    oh16 = jnp.concatenate([onehot_t, jnp.zeros_like(onehot_t)], axis=0).astype(BF16)
    rank_row = jnp.dot(oh16, triu_ref[...], preferred_element_type=F32)[0:SUBLANE]
    rrow_ref[0] = jnp.where(onehot_t > 0.0, rank_row, -1.0)
    cnt_ref[0] = jnp.broadcast_to(jnp.sum(onehot_t, axis=1, keepdims=True), (SUBLANE, LANE))
    lane = lax.broadcasted_iota(jnp.int32, (tm, LANE), 1)
    for g in range(N_GROUPS):
        shifted = comb if g == 0 else pltpu.roll(comb, LANE - g * E_PER_G, axis=1)
        comb_ref[:, g * LANE:(g + 1) * LANE] = jnp.where(lane < E_PER_G, shifted, 0.0)


def _mix_call(attn2d, uc2d, gates, x2d, ada_l, woa, wco, wout, g2, wr, br, S):
    T, D = x2d.shape
    tm = TM_MIX
    tpb = S // tm
    triu = jnp.tri(tm, tm, -1, dtype=BF16).T
    const = lambda shape: pl.BlockSpec(shape, lambda i: (0,) * len(shape))
    row = lambda w: pl.BlockSpec((tm, w), lambda i: (i, 0))
    return pl.pallas_call(
        _mix_kernel,
        grid=(T // tm,),
        in_specs=[
            row(N_HEADS * V_DIM),
            row(CONV_C),
            row(2 * D),
            row(D),
            pl.BlockSpec((1, 6, D), lambda i: (i // tpb, 0, 0)),
            const((N_HEADS * V_DIM, D)),
            const((CONV_C, D)),
            const((D, D)),
            const((1, D)),
            const((ROUTER_ROWS, D)),
            const((ROUTER_ROWS, 1)),
            const((tm, tm)),
        ],
        out_specs=[row(D), row(D), row(N_GROUPS * LANE),
                   pl.BlockSpec((1, SUBLANE, tm), lambda i: (i, 0, 0)),
                   pl.BlockSpec((1, SUBLANE, LANE), lambda i: (i, 0, 0))],
        out_shape=[jax.ShapeDtypeStruct((T, D), F32),
                   jax.ShapeDtypeStruct((T, D), BF16),
                   jax.ShapeDtypeStruct((T, N_GROUPS * LANE), F32),
                   jax.ShapeDtypeStruct((T // tm, SUBLANE, tm), F32),
                   jax.ShapeDtypeStruct((T // tm, SUBLANE, LANE), F32)],
        compiler_params=_cparams(("arbitrary",)),
        name="mixer_merge",
    )(attn2d, uc2d, gates, x2d, ada_l, woa, wco, wout, g2, wr, br, triu)


def _moe_kernel(cnt_ref, h_ref, comb_ref, rrow_ref, x1_ref, ada_ref, wg_ref, wu_ref, wd_ref, gf_ref,
                o_ref, acc_ref, *, final_norm):
    i = pl.program_id(0)
    tm = h_ref.shape[0]
    acc_ref[...] = jnp.zeros_like(acc_ref)
    row_id = lax.broadcasted_iota(jnp.int32, (MOE_CHUNK, tm), 0).astype(F32)

    for g in range(N_GROUPS):
        rank_row = rrow_ref[0, g:g + 1, :]
        cw = comb_ref[:, g * LANE:(g + 1) * LANE]
        cw_hi = cw.astype(BF16)
        cw_lo = (cw - cw_hi.astype(F32)).astype(BF16)
        cw2 = jnp.concatenate([cw_hi, cw_lo], axis=1)

        def chunk(c, carry, g=g, rank_row=rank_row, cw2=cw2):
            r0 = (c * MOE_CHUNK).astype(F32)
            sel = jnp.where(row_id + r0 == rank_row, 1.0, 0.0).astype(BF16)
            hc = jnp.dot(sel, h_ref[...], preferred_element_type=F32).astype(BF16)
            cwc = jnp.dot(sel, cw2, preferred_element_type=F32)
            cwc = cwc[:, :LANE] + cwc[:, LANE:]
            a = jnp.dot(hc, wg_ref[g], preferred_element_type=F32)
            u = jnp.dot(hc, wu_ref[g], preferred_element_type=F32)
            parts = []
            for e in range(E_PER_G):
                sl = slice(e * D_EXPERT, (e + 1) * D_EXPERT)
                ae = a[:, sl]
                ce = jnp.broadcast_to(cwc[:, e:e + 1], (MOE_CHUNK, D_EXPERT))
                parts.append((ae * jax.nn.sigmoid(ae) * u[:, sl] * ce).astype(BF16))
            y = jnp.dot(jnp.concatenate(parts, axis=-1), wd_ref[g], preferred_element_type=F32)
            acc_ref[...] += lax.dot_general(sel, y.astype(BF16), (((0,), (0,)), ((), ())),
                                            preferred_element_type=F32)
            return carry

        n = cnt_ref[i, g]
        n_chunks = sum((n > k * MOE_CHUNK).astype(jnp.int32) for k in range(pl.cdiv(tm, MOE_CHUNK)))
        lax.fori_loop(0, n_chunks, chunk, 0)

    gt2 = ada_ref[0, 5:6, :]
    x2 = x1_ref[...] + gt2 * acc_ref[...]
    if final_norm:
        x2 = _rms(x2, gf_ref[...])
    o_ref[...] = x2


def _moe_call(cnt, h2, comb, rrow, x1, ada_l, wg, wu, wd, gf, S, final_norm):
    T, D = x1.shape
    tm = TM_MIX
    tpb = S // tm
    EF = E_PER_G * D_EXPERT
    resident = lambda shape: pl.BlockSpec(shape, lambda i, c: (0,) * len(shape),
                                          pipeline_mode=pl.Buffered(1))
    return pl.pallas_call(
        functools.partial(_moe_kernel, final_norm=final_norm),
        grid_spec=pltpu.PrefetchScalarGridSpec(
            num_scalar_prefetch=1,
            grid=(T // tm,),
            in_specs=[
                pl.BlockSpec((tm, D), lambda i, c: (i, 0)),
                pl.BlockSpec((tm, N_GROUPS * LANE), lambda i, c: (i, 0)),
                pl.BlockSpec((1, SUBLANE, tm), lambda i, c: (i, 0, 0)),
                pl.BlockSpec((tm, D), lambda i, c: (i, 0)),
                pl.BlockSpec((1, 6, D), lambda i, c: (i // tpb, 0, 0)),
                resident((N_GROUPS, D, EF)),
                resident((N_GROUPS, D, EF)),
                resident((N_GROUPS, EF, D)),
                pl.BlockSpec((1, D), lambda i, c: (0, 0)),
            ],
            out_specs=pl.BlockSpec((tm, D), lambda i, c: (i, 0)),
            scratch_shapes=[pltpu.VMEM((tm, D), F32)],
        ),
        out_shape=jax.ShapeDtypeStruct((T, D), F32),
        compiler_params=_cparams(("arbitrary",)),
        name="moe_experts",
    )(cnt, h2, comb, rrow, x1, ada_l, wg, wu, wd, gf)


def _pad_heads(w, width):
    k = w.shape[0]
    w = w.reshape(k, N_HEADS, width)
    w = jnp.pad(w, ((0, 0), (0, 0), (0, HEAD_PAD - width)))
    return w.reshape(k, N_HEADS * HEAD_PAD)


def _layout_w_in(w):
    d = w.shape[0]
    sp_kr = Q_LORA + KV_LORA
    sp_conv = sp_kr + QK_ROPE
    zeros = lambda n: jnp.zeros((d, n), w.dtype)
    return jnp.concatenate(
        [w[:, :sp_kr], zeros(KPE_OFF), w[:, sp_kr:sp_conv], zeros(LANE - KPE_OFF - QK_ROPE), w[:, sp_conv:]],
        axis=1)


def kernel(x, c, positions, ada_w, ada_b, norm1_g, norm2_g, w_in, q_norm_g, w_uq, kv_norm_g, w_ukv,
           w_o_attn, conv_w, conv_b, conv_ln_g, conv_ln_b, w_conv_out, w_out, router_group_w,
           router_group_b, router_expert_w, router_expert_b, expert_w_gate, expert_w_up,
           expert_w_down, final_norm_g):
    B, S, D = x.shape
    L = ada_w.shape[0]
    T = B * S

    c_pad = jnp.pad(c, ((0, 8 - B), (0, 0)))
    ada = _ada_call(c_pad, ada_w, ada_b)[:, :B].reshape(L, B, 6, D)

    inv_freq = ROPE_BASE ** (-jnp.arange(0, QK_ROPE, 2, dtype=F32) / QK_ROPE)
    freq_row = jnp.zeros((1, LANE), F32)
    freq_row = freq_row.at[0, KPE_OFF:KPE_OFF + QK_ROPE // 2].set(inv_freq)
    freq_row = freq_row.at[0, KPE_OFF + QK_ROPE // 2:KPE_OFF + QK_ROPE].set(inv_freq)
    tc, ts1, ts2 = _rope_call(positions.reshape(T, 1), freq_row)

    x2d = x.reshape(T, D)
    for l in range(L):
        w_in_p = _layout_w_in(w_in[l]).astype(BF16)
        w_uq_p = _pad_heads(w_uq[l], QK_DIM).astype(BF16)
        wkv = w_ukv[l].reshape(KV_LORA, N_HEADS, QK_NOPE + V_DIM)
        w_uk_p = _pad_heads(wkv[:, :, :QK_NOPE].reshape(KV_LORA, -1), QK_NOPE).astype(BF16)
        w_uvt = jnp.pad(wkv[:, :, QK_NOPE:], ((0, 0), (0, 0), (0, VT_ROWS - V_DIM)))
        w_uvt_p = w_uvt.reshape(KV_LORA, N_HEADS * VT_ROWS).T.astype(BF16)
        cw = jnp.pad(conv_w[l], ((0, HALO - CONV_K), (0, 0)))
        q, k, vt, uc, gates = _inproj_call(
            x2d, ada[l], norm1_g[l].reshape(1, D), w_in_p, q_norm_g[l].reshape(1, -1), w_uq_p,
            kv_norm_g[l].reshape(1, -1), w_uk_p, w_uvt_p, tc, ts1, ts2,
            cw, conv_b[l].reshape(1, -1), conv_ln_g[l].reshape(1, -1), conv_ln_b[l].reshape(1, -1), B, S)

        attn = _attn_call(q, k, vt).reshape(T, N_HEADS * V_DIM)

        gpad = SUBLANE - N_GROUPS
        epad = ROUTER_ROWS - SUBLANE - N_GROUPS * E_PER_G
        w_r = jnp.concatenate([router_group_w[l].T, jnp.zeros((gpad, D), F32), router_expert_w[l].T,
                               jnp.zeros((epad, D), F32)], axis=0).astype(BF16)
        b_r = jnp.concatenate([router_group_b[l], jnp.zeros((gpad,), F32), router_expert_b[l],
                               jnp.zeros((epad,), F32)]).reshape(ROUTER_ROWS, 1)
        x1, h2, comb, rrow, cnt = _mix_call(
            attn, uc, gates, x2d, ada[l], w_o_attn[l].astype(BF16), w_conv_out[l].astype(BF16),
            w_out[l].astype(BF16), norm2_g[l].reshape(1, D), w_r, b_r, S)

        EF = E_PER_G * D_EXPERT
        wg = expert_w_gate[l].transpose(0, 2, 1, 3).reshape(N_GROUPS, D, EF).astype(BF16)
        wu = expert_w_up[l].transpose(0, 2, 1, 3).reshape(N_GROUPS, D, EF).astype(BF16)
        wd = expert_w_down[l].reshape(N_GROUPS, EF, D).astype(BF16)
        cnt_i = cnt[:, :N_GROUPS, 0].astype(jnp.int32)
        x2d = _moe_call(cnt_i, h2, comb, rrow, x1, ada[l], wg, wu, wd, final_norm_g.reshape(1, D), S,
                        final_norm=(l == L - 1))
    return x2d.reshape(B, S, D)
```

```python
import functools
import math

import jax
import jax.numpy as jnp
from jax import lax
from jax.experimental import pallas as pl
from jax.experimental.pallas import tpu as pltpu

D_MODEL = 1024
N_HEADS = 8
QK_NOPE = 64
QK_ROPE = 32
V_DIM = 64
QK_DIM = QK_NOPE + QK_ROPE
Q_LORA = 384
KV_LORA = 256
ROPE_BASE = 10000.0
CONV_C = 512
CONV_K = 31
N_GROUPS = 4
E_PER_G = 8
D_EXPERT = 128
EPS = 1e-6

LANE = 128
SUBLANE = 8
CONV_TCHUNK = 128
HEAD_PAD = LANE
KPE_OFF = QK_NOPE
VT_ROWS = 80
ROUTER_ROWS = 48
HALO = 32
VMEM_LIMIT = 56 * 1024 * 1024

C_QLAT = 0
C_KVLAT = Q_LORA
C_KPE = Q_LORA + KV_LORA
C_GLU_A = C_KPE + LANE
C_GLU_B = C_GLU_A + CONV_C
C_GATE = C_GLU_B + CONV_C
IN_COLS_PAD = C_GATE + 2 * D_MODEL

TM_IN = 512
TM_MIX = 512
MOE_CHUNK = 192
TQ = 512
TK = 256

BF16 = jnp.bfloat16
F32 = jnp.float32


def _cparams(sem, flags=None):
    return pltpu.CompilerParams(dimension_semantics=sem, vmem_limit_bytes=VMEM_LIMIT, flags=flags)


def _ada_kernel(c_ref, w_ref, b_ref, o_ref):
    c = c_ref[...]
    c_act = c * jax.nn.sigmoid(c)
    o_ref[0] = jnp.dot(c_act, w_ref[0], preferred_element_type=F32,
                       precision=lax.Precision.HIGHEST) + b_ref[0]


def _ada_call(c_pad, ada_w, ada_b):
    L, D, N = ada_w.shape
    tn = 1536
    return pl.pallas_call(
        _ada_kernel,
        grid=(L, N // tn),
        in_specs=[
            pl.BlockSpec((8, D), lambda l, j: (0, 0)),
            pl.BlockSpec((1, D, tn), lambda l, j: (l, 0, j)),
            pl.BlockSpec((1, 1, tn), lambda l, j: (l, 0, j)),
        ],
        out_specs=pl.BlockSpec((1, 8, tn), lambda l, j: (l, 0, j)),
        out_shape=jax.ShapeDtypeStruct((L, 8, N), F32),
        compiler_params=_cparams(("arbitrary", "arbitrary")),
        name="ada_ln",
    )(c_pad, ada_w, ada_b.reshape(L, 1, N))


def _rope_kernel(pos_ref, freq_ref, c_ref, s1_ref, s2_ref):
    pos = pos_ref[...].astype(F32)
    ang = pos * freq_ref[...]
    lane = lax.broadcasted_iota(jnp.int32, ang.shape, 1)
    cosv = jnp.cos(ang)
    sinv = jnp.sin(ang)
    is_x1 = (lane >= KPE_OFF) & (lane < KPE_OFF + QK_ROPE // 2)
    is_x2 = (lane >= KPE_OFF + QK_ROPE // 2) & (lane < KPE_OFF + QK_ROPE)
    c_ref[...] = jnp.where(is_x1 | is_x2, cosv, jnp.where(lane < KPE_OFF, 1.0, 0.0))
    s1_ref[...] = jnp.where(is_x2, sinv, 0.0)
    s2_ref[...] = jnp.where(is_x1, -sinv, 0.0)


def _rope_call(pos_col, freq_row):
    T = pos_col.shape[0]
    tm = 2048
    spec = pl.BlockSpec((tm, LANE), lambda i: (i, 0))
    return pl.pallas_call(
        _rope_kernel,
        grid=(T // tm,),
        in_specs=[pl.BlockSpec((tm, 1), lambda i: (i, 0)),
                  pl.BlockSpec((1, LANE), lambda i: (0, 0))],
        out_specs=[spec, spec, spec],
        out_shape=[jax.ShapeDtypeStruct((T, LANE), F32)] * 3,
        compiler_params=_cparams(("arbitrary",)),
        name="rope_tables",
    )(pos_col, freq_row)


def _rms(x, g):
    ms = jnp.mean(x * x, axis=-1, keepdims=True)
    return x * lax.rsqrt(ms + EPS) * g


def _rope_apply(x, c, s1, s2):
    half = QK_ROPE // 2
    return x * c + pltpu.roll(x, half, axis=1) * s1 + pltpu.roll(x, LANE - half, axis=1) * s2


def _conv_ln_silu(buf_ref, cw_ref, cb_ref, lng_ref, lnb_ref, tm):
    base = HALO - (CONV_K - 1)
    wrows = CONV_TCHUNK + HALO
    pieces = []
    for t0 in range(0, tm, CONV_TCHUNK):
        row = []
        for c0 in range(0, CONV_C, LANE):
            cs = slice(c0, c0 + LANE)
            win = buf_ref[t0:t0 + wrows, cs]
            acc = jnp.broadcast_to(cb_ref[:, cs], (CONV_TCHUNK, LANE))
            for r in range(SUBLANE):
                shifted = win if r == 0 else pltpu.roll(win, wrows - r, axis=0)
                for o in range(base, base + CONV_K):
                    if o % SUBLANE == r:
                        acc = acc + shifted[o - r:o - r + CONV_TCHUNK] * cw_ref[o - base:o - base + 1, cs]
            row.append(acc)
        pieces.append(jnp.concatenate(row, axis=1))
    acc = jnp.concatenate(pieces, axis=0)
    mu = jnp.mean(acc, axis=-1, keepdims=True)
    xc = acc - mu
    var = jnp.mean(xc * xc, axis=-1, keepdims=True)
    ln = xc * lax.rsqrt(var + EPS) * lng_ref[...] + lnb_ref[...]
    return ln * jax.nn.sigmoid(ln)


def _inproj_kernel(x_ref, ada_ref, g1_ref, win_ref, qg_ref, wuq_ref, kvg_ref, wuk_ref, wuvt_ref,
                   c_ref, s1_ref, s2_ref, cw_ref, cb_ref, lng_ref, lnb_ref,
                   q_ref, k_ref, vt_ref, uc_ref, gate_ref, buf_ref, *, tiles_per_batch):
    tm = x_ref.shape[0]
    first = (pl.program_id(0) % tiles_per_batch) == 0

    @pl.when(first)
    def _():
        buf_ref[0:HALO, :] = jnp.zeros((HALO, CONV_C), F32)

    @pl.when(jnp.logical_not(first))
    def _():
        buf_ref[0:HALO, :] = buf_ref[tm:tm + HALO, :]

    x = x_ref[...]
    sh1 = ada_ref[0, 0:1, :]
    sc1 = ada_ref[0, 1:2, :]
    h = _rms(x, g1_ref[...]) * (1.0 + sc1) + sh1
    z = jnp.dot(h.astype(BF16), win_ref[...], preferred_element_type=F32)

    qn = _rms(z[:, C_QLAT:C_QLAT + Q_LORA], qg_ref[...]).astype(BF16)
    kvn = _rms(z[:, C_KVLAT:C_KVLAT + KV_LORA], kvg_ref[...]).astype(BF16)
    q = jnp.dot(qn, wuq_ref[...], preferred_element_type=F32)
    kk = jnp.dot(kvn, wuk_ref[...], preferred_element_type=F32)
    vvt = lax.dot_general(wuvt_ref[...], kvn, (((1,), (1,)), ((), ())), preferred_element_type=F32)
    vrow = lax.broadcasted_iota(jnp.int32, (VT_ROWS, 1), 0)
    ones_row = jnp.where(vrow == V_DIM, 1.0, 0.0)

    c = c_ref[...]
    s1 = s1_ref[...]
    s2 = s2_ref[...]
    qs = (QK_DIM ** -0.5) * math.log2(math.e)
    cq, s1q, s2q = c * qs, s1 * qs, s2 * qs
    kpe = _rope_apply(z[:, C_KPE:C_KPE + LANE], c, s1, s2)
    for hh in range(N_HEADS):
        sl = slice(hh * HEAD_PAD, (hh + 1) * HEAD_PAD)
        q_ref[0, hh] = _rope_apply(q[:, sl], cq, s1q, s2q).astype(BF16)
        k_ref[0, hh] = (kk[:, sl] + kpe).astype(BF16)
        vth = (vvt[hh * VT_ROWS:(hh + 1) * VT_ROWS, :] + ones_row).astype(BF16)
        for j in range(vt_ref.shape[2]):
            vt_ref[0, hh, j] = vth[:, j * TK:(j + 1) * TK]

    gate_ref[...] = jax.nn.sigmoid(z[:, C_GATE:C_GATE + 2 * D_MODEL]).astype(BF16)

    buf_ref[HALO:HALO + tm, :] = z[:, C_GLU_A:C_GLU_A + CONV_C] * jax.nn.sigmoid(z[:, C_GLU_B:C_GLU_B + CONV_C])
    uc_ref[...] = _conv_ln_silu(buf_ref, cw_ref, cb_ref, lng_ref, lnb_ref, tm).astype(BF16)


def _inproj_call(x2d, ada_l, g1, w_in_p, qg, w_uq_p, kvg, w_uk_p, w_uvt_p, tc, ts1, ts2,
                 cw, cb, lng, lnb, B, S):
    T, D = x2d.shape
    tm = TM_IN
    tpb = S // tm
    kpt = tm // TK
    const = lambda shape: pl.BlockSpec(shape, lambda i: (0,) * len(shape))
    head_spec = pl.BlockSpec((1, N_HEADS, tm, HEAD_PAD), lambda i: (i // tpb, 0, i % tpb, 0))
    head_shape = jax.ShapeDtypeStruct((B, N_HEADS, S, HEAD_PAD), BF16)
    tab = pl.BlockSpec((tm, LANE), lambda i: (i, 0))
    return pl.pallas_call(
        functools.partial(_inproj_kernel, tiles_per_batch=tpb),
        grid=(T // tm,),
        in_specs=[
            pl.BlockSpec((tm, D), lambda i: (i, 0)),
            pl.BlockSpec((1, 6, D), lambda i: (i // tpb, 0, 0)),
            const((1, D)),
            const((D, IN_COLS_PAD)),
            const((1, Q_LORA)),
            const((Q_LORA, N_HEADS * HEAD_PAD)),
            const((1, KV_LORA)),
            const((KV_LORA, N_HEADS * HEAD_PAD)),
            const((N_HEADS * VT_ROWS, KV_LORA)),
            tab, tab, tab,
            const((HALO, CONV_C)), const((1, CONV_C)), const((1, CONV_C)), const((1, CONV_C)),
        ],
        out_specs=[head_spec, head_spec,
                   pl.BlockSpec((1, N_HEADS, kpt, VT_ROWS, TK), lambda i: (i // tpb, 0, i % tpb, 0, 0)),
                   pl.BlockSpec((tm, CONV_C), lambda i: (i, 0)),
                   pl.BlockSpec((tm, 2 * D_MODEL), lambda i: (i, 0))],
        out_shape=[head_shape, head_shape,
                   jax.ShapeDtypeStruct((B, N_HEADS, S // TK, VT_ROWS, TK), BF16),
                   jax.ShapeDtypeStruct((T, CONV_C), BF16),
                   jax.ShapeDtypeStruct((T, 2 * D_MODEL), BF16)],
        scratch_shapes=[pltpu.VMEM((HALO + tm, CONV_C), F32)],
        compiler_params=_cparams(("arbitrary",)),
        name="in_proj",
    )(x2d, ada_l, g1, w_in_p, qg, w_uq_p, kvg, w_uk_p, w_uvt_p, tc, ts1, ts2, cw, cb, lng, lnb)


HEADS_PER_STEP = 8


def _attn_kernel(q_ref, k_ref, vt_ref, o_ref, *scratch):
    qi = pl.program_id(2)
    heads = range(HEADS_PER_STEP)
    assert TQ == 2 * TK
    hs = HEADS_PER_STEP
    s_ref = [scratch[2 * hh:2 * hh + 2] for hh in heads]
    p_ref = [scratch[2 * hs + 2 * hh:2 * hs + 2 * hh + 2] for hh in heads]
    acc_ref = scratch[4 * hs:5 * hs]

    def qk(kb, slot, qlo=0):
        for hh in heads:
            k = k_ref[0, hh, pl.ds(kb * TK, TK), :]
            s_ref[hh][slot][:, qlo:] = lax.dot_general(k, q_ref[0, hh, qlo:, :], (((1,), (1,)), ((), ())),
                                                       preferred_element_type=F32)

    def pv(kb, slot, alpha, qlo=0):
        for hh in heads:
            acc_ref[hh][:, qlo:] = (acc_ref[hh][:, qlo:] * alpha[hh]
                                    + jnp.dot(vt_ref[0, hh, kb], p_ref[hh][slot][:, qlo:],
                                              preferred_element_type=F32))

    def softmax(slot, m, mask=None, qlo=0):
        m_out, alpha = [], []
        for hh in heads:
            s = s_ref[hh][slot][:, qlo:]
            if mask is not None:
                s = jnp.where(mask[:, qlo:], s, -jnp.inf)
            m_old = m[hh][:, qlo:]
            m_new = jnp.maximum(m_old, jnp.max(s, axis=0, keepdims=True))
            p_ref[hh][slot][:, qlo:] = jnp.exp2(s - m_new).astype(BF16)
            alpha.append(jnp.exp2(m_old - m_new))
            m_out.append(m_new)
        return tuple(m_out), tuple(alpha)

    for hh in heads:
        acc_ref[hh][...] = jnp.zeros((VT_ROWS, TQ), F32)
        p_ref[hh][1][...] = jnp.zeros((TK, TQ), BF16)
    m0 = tuple(jnp.full((1, TQ), -jnp.inf, F32) for _ in heads)
    one = tuple(jnp.ones((1, TQ), F32) for _ in heads)
    qk(0, 0)

    def pair(a, carry):
        m, alpha1 = carry
        qk(a + 1, 1)
        pv(jnp.maximum(a - 1, 0), 1, alpha1)
        m, alpha0 = softmax(0, m)
        qk(a + 2, 0)
        pv(a, 0, alpha0)
        m, alpha1 = softmax(1, m)
        return m, alpha1

    m, alpha1 = lax.fori_loop(0, qi, lambda i, c: pair(2 * i, c), (m0, one))
    a = 2 * qi
    key = lax.broadcasted_iota(jnp.int32, (TK, TQ), 0)
    qry = lax.broadcasted_iota(jnp.int32, (TK, TQ), 1)
    qk(a + 1, 1, qlo=TK)
    pv(jnp.maximum(a - 1, 0), 1, alpha1)
    m, alpha0 = softmax(0, m, key <= qry)
    pv(a, 0, alpha0)
    _, alpha1 = softmax(1, m, key + TK <= qry, qlo=TK)
    pv(a + 1, 1, alpha1, qlo=TK)
    outs = []
    for hh in heads:
        acc = acc_ref[hh][...]
        o_t = acc[:V_DIM, :] / acc[V_DIM:V_DIM + 1, :]
        outs.append(o_t.T)
    o_ref[0] = jnp.concatenate(outs, axis=-1).astype(BF16)


def _attn_call(q, k, vt):
    B, H, S, _ = q.shape
    nq = S // TQ
    hs = HEADS_PER_STEP
    return pl.pallas_call(
        _attn_kernel,
        grid=(B, H // hs, nq),
        in_specs=[
            pl.BlockSpec((1, hs, TQ, HEAD_PAD), lambda b, h, i: (b, h, i, 0)),
            pl.BlockSpec((1, hs, S, HEAD_PAD), lambda b, h, i: (b, h, 0, 0),
                         pipeline_mode=pl.Buffered(1)),
            pl.BlockSpec((1, hs, S // TK, VT_ROWS, TK), lambda b, h, i: (b, h, 0, 0, 0),
                         pipeline_mode=pl.Buffered(1)),
        ],
        out_specs=pl.BlockSpec((1, TQ, hs * V_DIM), lambda b, h, i: (b, i, h)),
        out_shape=jax.ShapeDtypeStruct((B, S, H * V_DIM), BF16),
        scratch_shapes=([pltpu.VMEM((TK, TQ), F32)] * (2 * hs) + [pltpu.VMEM((TK, TQ), BF16)] * (2 * hs)
                        + [pltpu.VMEM((VT_ROWS, TQ), F32)] * hs),
        compiler_params=_cparams(("arbitrary", "arbitrary", "arbitrary")),
        name="mla_attention",
    )(q, k, vt)


def _mix_kernel(attn_ref, uc_ref, gate_ref, x_ref, ada_ref, woa_ref, wco_ref, wout_ref, g2_ref,
                wr_ref, br_ref, triu_ref, x1_ref, h2_ref, comb_ref, rrow_ref, cnt_ref):
    y_conv = jnp.dot(uc_ref[...], wco_ref[...], preferred_element_type=F32)
    y_attn = jnp.dot(attn_ref[...], woa_ref[...], preferred_element_type=F32)

    g_attn = gate_ref[:, 0:D_MODEL].astype(F32)
    g_conv = gate_ref[:, D_MODEL:2 * D_MODEL].astype(F32)
    merged = g_attn * y_attn + g_conv * y_conv
    y = jnp.dot(merged.astype(BF16), wout_ref[...], preferred_element_type=F32)
    gt1 = ada_ref[0, 2:3, :]
    x1 = x_ref[...] + gt1 * y
    x1_ref[...] = x1

    sh2 = ada_ref[0, 3:4, :]
    sc2 = ada_ref[0, 4:5, :]
    h2 = (_rms(x1, g2_ref[...]) * (1.0 + sc2) + sh2).astype(BF16)
    h2_ref[...] = h2

    tm = h2.shape[0]
    logits_t = lax.dot_general(wr_ref[...], h2, (((1,), (1,)), ((), ())),
                               preferred_element_type=F32) + br_ref[...]
    row = lax.broadcasted_iota(jnp.int32, (SUBLANE, tm), 0)
    big = jnp.int32(1 << 20)
    neg = -jnp.inf
    lg = jnp.where(row < N_GROUPS, logits_t[0:SUBLANE], neg)
    gmax = jnp.max(lg, axis=0, keepdims=True)
    g_val = 1.0 / jnp.sum(jnp.exp(lg - gmax), axis=0, keepdims=True)
    g_idx = jnp.min(jnp.where(lg == gmax, row, big), axis=0, keepdims=True)
    le = logits_t[SUBLANE:2 * SUBLANE]
    for g in range(1, N_GROUPS):
        le = jnp.where(g_idx == g, logits_t[(g + 1) * SUBLANE:(g + 2) * SUBLANE], le)
    m1 = jnp.max(le, axis=0, keepdims=True)
    i1 = jnp.min(jnp.where(le == m1, row, big), axis=0, keepdims=True)
    le2 = jnp.where(row == i1, neg, le)
    m2 = jnp.max(le2, axis=0, keepdims=True)
    i2 = jnp.min(jnp.where(le2 == m2, row, big), axis=0, keepdims=True)
    z_e = jnp.sum(jnp.exp(le - m1), axis=0, keepdims=True)
    p1 = 1.0 / z_e
    p2 = jnp.exp(m2 - m1) / z_e
    w1 = g_val * (p1 / (p1 + p2))
    w2 = g_val * (p2 / (p1 + p2))
    slab = jnp.where(row == i1, w1, 0.0) + jnp.where(row == i2, w2, 0.0)
    comb_t = jnp.concatenate([jnp.where(g_idx == g, slab, 0.0) for g in range(N_GROUPS)]
                             + [jnp.zeros((LANE - N_GROUPS * E_PER_G, tm), F32)], axis=0)
    comb = comb_t.T
    onehot_t = jnp.where(row == g_idx, 1.0, 0.0)
    oh16 = jnp.concatenate([onehot_t, jnp.zeros_like(onehot_t)], axis=0).astype(BF16)
    rank_row = jnp.dot(oh16, triu_ref[...], preferred_element_type=F32)[0:SUBLANE]
    rrow_ref[0] = jnp.where(onehot_t > 0.0, rank_row, -1.0)
    cnt_ref[0] = jnp.broadcast_to(jnp.sum(onehot_t, axis=1, keepdims=True), (SUBLANE, LANE))
    comb_ref[...] = comb


def _mix_call(attn2d, uc2d, gates, x2d, ada_l, woa, wco, wout, g2, wr, br, S):
    T, D = x2d.shape
    tm = TM_MIX
    tpb = S // tm
    triu = jnp.tri(tm, tm, -1, dtype=BF16).T
    const = lambda shape: pl.BlockSpec(shape, lambda i: (0,) * len(shape))
    row = lambda w: pl.BlockSpec((tm, w), lambda i: (i, 0))
    return pl.pallas_call(
        _mix_kernel,
        grid=(T // tm,),
        in_specs=[
            row(N_HEADS * V_DIM),
            row(CONV_C),
            row(2 * D),
            row(D),
            pl.BlockSpec((1, 6, D), lambda i: (i // tpb, 0, 0)),
            const((N_HEADS * V_DIM, D)),
            const((CONV_C, D)),
            const((D, D)),
            const((1, D)),
            const((ROUTER_ROWS, D)),
            const((ROUTER_ROWS, 1)),
            const((tm, tm)),
        ],
        out_specs=[row(D), row(D), row(LANE),
                   pl.BlockSpec((1, SUBLANE, tm), lambda i: (i, 0, 0)),
                   pl.BlockSpec((1, SUBLANE, LANE), lambda i: (i, 0, 0))],
        out_shape=[jax.ShapeDtypeStruct((T, D), F32),
                   jax.ShapeDtypeStruct((T, D), BF16),
                   jax.ShapeDtypeStruct((T, LANE), F32),
                   jax.ShapeDtypeStruct((T // tm, SUBLANE, tm), F32),
                   jax.ShapeDtypeStruct((T // tm, SUBLANE, LANE), F32)],
        compiler_params=_cparams(("arbitrary",)),
        name="mixer_merge",
    )(attn2d, uc2d, gates, x2d, ada_l, woa, wco, wout, g2, wr, br, triu)


def _moe_kernel(cnt_ref, h_ref, comb_ref, rrow_ref, x1_ref, ada_ref, wg_ref, wu_ref, wd_ref, gf_ref,
                o_ref, acc_ref, *, final_norm):
    i = pl.program_id(0)
    tm = h_ref.shape[0]
    acc_ref[...] = jnp.zeros_like(acc_ref)
    row_id = lax.broadcasted_iota(jnp.int32, (MOE_CHUNK, tm), 0).astype(F32)
    cw = comb_ref[...]
    cw_hi = cw.astype(BF16)
    cw_lo = (cw - cw_hi.astype(F32)).astype(BF16)
    cw2 = jnp.concatenate([cw_hi, cw_lo], axis=1)

    for g in range(N_GROUPS):
        rank_row = rrow_ref[0, g:g + 1, :]

        def chunk(c, carry, g=g, rank_row=rank_row):
            r0 = (c * MOE_CHUNK).astype(F32)
            sel = jnp.where(row_id + r0 == rank_row, 1.0, 0.0).astype(BF16)
            hc = jnp.dot(sel, h_ref[...], preferred_element_type=F32).astype(BF16)
            cwc = jnp.dot(sel, cw2, preferred_element_type=F32)
            cwc = cwc[:, :LANE] + cwc[:, LANE:]
            a = jnp.dot(hc, wg_ref[g], preferred_element_type=F32)
            u = jnp.dot(hc, wu_ref[g], preferred_element_type=F32)
            parts = []
            for e in range(E_PER_G):
                sl = slice(e * D_EXPERT, (e + 1) * D_EXPERT)
                ae = a[:, sl]
                le = g * E_PER_G + e
                ce = jnp.broadcast_to(cwc[:, le:le + 1], (MOE_CHUNK, D_EXPERT))
                parts.append((ae * jax.nn.sigmoid(ae) * u[:, sl] * ce).astype(BF16))
            y = jnp.dot(jnp.concatenate(parts, axis=-1), wd_ref[g], preferred_element_type=F32)
            acc_ref[...] += lax.dot_general(sel, y.astype(BF16), (((0,), (0,)), ((), ())),
                                            preferred_element_type=F32)
            return carry

        n = cnt_ref[i, g]
        n_chunks = sum((n > k * MOE_CHUNK).astype(jnp.int32) for k in range(pl.cdiv(tm, MOE_CHUNK)))
        lax.fori_loop(0, n_chunks, chunk, 0)

    gt2 = ada_ref[0, 5:6, :]
    x2 = x1_ref[...] + gt2 * acc_ref[...]
    if final_norm:
        x2 = _rms(x2, gf_ref[...])
    o_ref[...] = x2


def _moe_call(cnt, h2, comb, rrow, x1, ada_l, wg, wu, wd, gf, S, final_norm):
    T, D = x1.shape
    tm = TM_MIX
    tpb = S // tm
    EF = E_PER_G * D_EXPERT
    resident = lambda shape: pl.BlockSpec(shape, lambda i, c: (0,) * len(shape),
                                          pipeline_mode=pl.Buffered(1))
    return pl.pallas_call(
        functools.partial(_moe_kernel, final_norm=final_norm),
        grid_spec=pltpu.PrefetchScalarGridSpec(
            num_scalar_prefetch=1,
            grid=(T // tm,),
            in_specs=[
                pl.BlockSpec((tm, D), lambda i, c: (i, 0)),
                pl.BlockSpec((tm, LANE), lambda i, c: (i, 0)),
                pl.BlockSpec((1, SUBLANE, tm), lambda i, c: (i, 0, 0)),
                pl.BlockSpec((tm, D), lambda i, c: (i, 0)),
                pl.BlockSpec((1, 6, D), lambda i, c: (i // tpb, 0, 0)),
                resident((N_GROUPS, D, EF)),
                resident((N_GROUPS, D, EF)),
                resident((N_GROUPS, EF, D)),
                pl.BlockSpec((1, D), lambda i, c: (0, 0)),
            ],
            out_specs=pl.BlockSpec((tm, D), lambda i, c: (i, 0)),
            scratch_shapes=[pltpu.VMEM((tm, D), F32)],
        ),
        out_shape=jax.ShapeDtypeStruct((T, D), F32),
        compiler_params=_cparams(("arbitrary",)),
        name="moe_experts",
    )(cnt, h2, comb, rrow, x1, ada_l, wg, wu, wd, gf)


def _pad_heads(w, width):
    k = w.shape[0]
    w = w.reshape(k, N_HEADS, width)
    w = jnp.pad(w, ((0, 0), (0, 0), (0, HEAD_PAD - width)))
    return w.reshape(k, N_HEADS * HEAD_PAD)


def _layout_w_in(w):
    d = w.shape[0]
    sp_kr = Q_LORA + KV_LORA
    sp_conv = sp_kr + QK_ROPE
    zeros = lambda n: jnp.zeros((d, n), w.dtype)
    return jnp.concatenate(
        [w[:, :sp_kr], zeros(KPE_OFF), w[:, sp_kr:sp_conv], zeros(LANE - KPE_OFF - QK_ROPE), w[:, sp_conv:]],
        axis=1)


def kernel(x, c, positions, ada_w, ada_b, norm1_g, norm2_g, w_in, q_norm_g, w_uq, kv_norm_g, w_ukv,
           w_o_attn, conv_w, conv_b, conv_ln_g, conv_ln_b, w_conv_out, w_out, router_group_w,
           router_group_b, router_expert_w, router_expert_b, expert_w_gate, expert_w_up,
           expert_w_down, final_norm_g):
    B, S, D = x.shape
    L = ada_w.shape[0]
    T = B * S

    c_pad = jnp.pad(c, ((0, 8 - B), (0, 0)))
    ada = _ada_call(c_pad, ada_w, ada_b)[:, :B].reshape(L, B, 6, D)

    inv_freq = ROPE_BASE ** (-jnp.arange(0, QK_ROPE, 2, dtype=F32) / QK_ROPE)
    freq_row = jnp.zeros((1, LANE), F32)
    freq_row = freq_row.at[0, KPE_OFF:KPE_OFF + QK_ROPE // 2].set(inv_freq)
    freq_row = freq_row.at[0, KPE_OFF + QK_ROPE // 2:KPE_OFF + QK_ROPE].set(inv_freq)
    tc, ts1, ts2 = _rope_call(positions.reshape(T, 1), freq_row)

    x2d = x.reshape(T, D)
    for l in range(L):
        w_in_p = _layout_w_in(w_in[l]).astype(BF16)
        w_uq_p = _pad_heads(w_uq[l], QK_DIM).astype(BF16)
        wkv = w_ukv[l].reshape(KV_LORA, N_HEADS, QK_NOPE + V_DIM)
        w_uk_p = _pad_heads(wkv[:, :, :QK_NOPE].reshape(KV_LORA, -1), QK_NOPE).astype(BF16)
        w_uvt = jnp.pad(wkv[:, :, QK_NOPE:], ((0, 0), (0, 0), (0, VT_ROWS - V_DIM)))
        w_uvt_p = w_uvt.reshape(KV_LORA, N_HEADS * VT_ROWS).T.astype(BF16)
        cw = jnp.pad(conv_w[l], ((0, HALO - CONV_K), (0, 0)))
        q, k, vt, uc, gates = _inproj_call(
            x2d, ada[l], norm1_g[l].reshape(1, D), w_in_p, q_norm_g[l].reshape(1, -1), w_uq_p,
            kv_norm_g[l].reshape(1, -1), w_uk_p, w_uvt_p, tc, ts1, ts2,
            cw, conv_b[l].reshape(1, -1), conv_ln_g[l].reshape(1, -1), conv_ln_b[l].reshape(1, -1), B, S)

        attn = _attn_call(q, k, vt).reshape(T, N_HEADS * V_DIM)

        gpad = SUBLANE - N_GROUPS
        epad = ROUTER_ROWS - SUBLANE - N_GROUPS * E_PER_G
        w_r = jnp.concatenate([router_group_w[l].T, jnp.zeros((gpad, D), F32), router_expert_w[l].T,
                               jnp.zeros((epad, D), F32)], axis=0).astype(BF16)
        b_r = jnp.concatenate([router_group_b[l], jnp.zeros((gpad,), F32), router_expert_b[l],
                               jnp.zeros((epad,), F32)]).reshape(ROUTER_ROWS, 1)
        x1, h2, comb, rrow, cnt = _mix_call(
            attn, uc, gates, x2d, ada[l], w_o_attn[l].astype(BF16), w_conv_out[l].astype(BF16),
            w_out[l].astype(BF16), norm2_g[l].reshape(1, D), w_r, b_r, S)

        EF = E_PER_G * D_EXPERT
        wg = expert_w_gate[l].transpose(0, 2, 1, 3).reshape(N_GROUPS, D, EF).astype(BF16)
        wu = expert_w_up[l].transpose(0, 2, 1, 3).reshape(N_GROUPS, D, EF).astype(BF16)
        wd = expert_w_down[l].reshape(N_GROUPS, EF, D).astype(BF16)
        cnt_i = cnt[:, :N_GROUPS, 0].astype(jnp.int32)
        x2d = _moe_call(cnt_i, h2, comb, rrow, x1, ada[l], wg, wu, wd, final_norm_g.reshape(1, D), S,
                        final_norm=(l == L - 1))
    return x2d.reshape(B, S, D)
```
